```python
import jax
import jax.numpy as jnp
from jax import lax
import numpy as np

D_MODEL = 1024
BATCH = 8
SEQ = 4096
DEPTH = 4

GRID_W = 64
CTX_LEN = 256
N_MIXERS = 2
RET_HEADS = 4
RET_QK_DIM = D_MODEL // RET_HEADS
RET_V_DIM = 2 * D_MODEL // RET_HEADS
RET_CHUNK = 128
NAT_HEAD_DIM = 32
NAT_HEADS = D_MODEL // NAT_HEAD_DIM
NAT_WIN_ROWS = 8
NAT_WIN_COLS = 16
MLP_HIDDEN = 4 * D_MODEL
ROPE_BASE = 10000.0
LN_EPS = 1e-5
DN_ALPHA = (2.0 * DEPTH) ** 0.25
DN_BETA = (8.0 * DEPTH) ** -0.25
N_RET_LAYERS = (DEPTH + 1) // 2
N_NAT_LAYERS = DEPTH // 2

kernel_name = 'hybrid_retention_natten_dit'


def layer_norm(x, g, b):
    xf = x.astype(jnp.float32)
    mu = jnp.mean(xf, axis=-1, keepdims=True)
    var = jnp.mean(jnp.square(xf - mu), axis=-1, keepdims=True)
    return ((xf - mu) * lax.rsqrt(var + LN_EPS) * g + b).astype(x.dtype)


def axial_rope_angles(n, axis_dim):
    t = jnp.arange(n)
    row = (t // GRID_W).astype(jnp.float32)
    col = (t % GRID_W).astype(jnp.float32)
    inv = 1.0 / (ROPE_BASE ** (jnp.arange(0, axis_dim, 2, dtype=jnp.float32) / axis_dim))
    return row[:, None] * inv, col[:, None] * inv


def rope_rotate(x, ang):
    cos = jnp.cos(ang).astype(x.dtype)
    sin = jnp.sin(ang).astype(x.dtype)
    x1, x2 = jnp.split(x, 2, axis=-1)
    return jnp.concatenate([x1 * cos - x2 * sin, x2 * cos + x1 * sin], axis=-1)


def axial_rope(x, ang_row, ang_col):
    xr, xc = jnp.split(x, 2, axis=-1)
    return jnp.concatenate([rope_rotate(xr, ang_row), rope_rotate(xc, ang_col)], axis=-1)


def retention_chunk_scan(q, k, v, log_gamma, state0, with_out):
    b, h, n, dk = q.shape
    dv = v.shape[-1]
    cs = RET_CHUNK
    nc = n // cs
    dt = v.dtype
    pos = jnp.arange(cs, dtype=jnp.float32)
    lg = log_gamma.astype(jnp.float32)[:, None]
    rel = pos[:, None] - pos[None, :]
    intra = jnp.where(rel >= 0, jnp.exp(lg[:, :, None] * jnp.maximum(rel, 0.0)), 0.0).astype(dt)
    q_dec = jnp.exp(lg * (pos + 1.0))[..., None].astype(dt)
    k_dec = jnp.exp(lg * (cs - 1.0 - pos))[..., None].astype(dt)
    chunk_dec = jnp.exp(lg * cs)[..., None].astype(dt)

    def to_chunks(t):
        return jnp.moveaxis(t.reshape(b, h, nc, cs, t.shape[-1]), 2, 0)

    def step(state, qkv):
        qc, kc, vc = qkv
        new_state = state * chunk_dec + jnp.einsum('bhsd,bhse->bhde', kc * k_dec, vc)
        if with_out:
            scores = jnp.einsum('bhcd,bhsd->bhcs', qc, kc) * intra
            out = (jnp.einsum('bhcs,bhse->bhce', scores, vc)
                   + jnp.einsum('bhcd,bhde->bhce', qc * q_dec, state))
            return new_state, out
        return new_state, None

    state, outs = lax.scan(step, state0, (to_chunks(q), to_chunks(k), to_chunks(v)))
    if with_out:
        outs = jnp.moveaxis(outs, 0, 2).reshape(b, h, n, dv)
    return state, outs


def retention_mixer(h_lat, h_ctx, w_in, w_o, decay_param, ang_row, ang_col, ctx_out):
    b = h_lat.shape[0]
    qk_w = RET_HEADS * RET_QK_DIM
    v_w = RET_HEADS * RET_V_DIM

    def heads(t, hd):
        return t.reshape(b, t.shape[1], RET_HEADS, hd).transpose(0, 2, 1, 3)

    def project(h, w):
        p = h @ w
        q = heads(p[..., :qk_w], RET_QK_DIM)
        k = heads(p[..., qk_w:2 * qk_w], RET_QK_DIM) * (RET_QK_DIM ** -0.5)
        v = heads(p[..., 2 * qk_w:2 * qk_w + v_w], RET_V_DIM)
        return q, k, v, p[..., 2 * qk_w + v_w:]

    def flip(t):
        return t[:, :, ::-1]

    def finish(o, g):
        of = o.astype(jnp.float32)
        mu = jnp.mean(of, axis=-1, keepdims=True)
        var = jnp.mean(jnp.square(of - mu), axis=-1, keepdims=True)
        on = ((of - mu) * lax.rsqrt(var + LN_EPS)).astype(o.dtype)
        on = on.transpose(0, 2, 1, 3).reshape(b, o.shape[2], v_w)
        return (on * jax.nn.silu(g)) @ w_o

    log_gamma = -jnp.exp(decay_param.astype(jnp.float32))
    w_ctx = w_in if ctx_out else w_in[:, :2 * qk_w + v_w]
    q_c, k_c, v_c, g_c = project(h_ctx, w_ctx)
    zeros = jnp.zeros((b, RET_HEADS, RET_QK_DIM, RET_V_DIM), v_c.dtype)
    s_f, o_cf = retention_chunk_scan(q_c, k_c, v_c, log_gamma[0], zeros, ctx_out)
    s_b, o_cb = retention_chunk_scan(flip(q_c), flip(k_c), flip(v_c), log_gamma[1], zeros, ctx_out)

    q_l, k_l, v_l, g_l = project(h_lat, w_in)
    q_l = axial_rope(q_l, ang_row, ang_col)
    k_l = axial_rope(k_l, ang_row, ang_col)
    _, o_lf = retention_chunk_scan(q_l, k_l, v_l, log_gamma[0], s_f, True)
    _, o_lb = retention_chunk_scan(flip(q_l), flip(k_l), flip(v_l), log_gamma[1], s_b, True)
    y_lat = finish(o_lf + flip(o_lb), g_l)
    y_ctx = finish(o_cf + flip(o_cb), g_c) if ctx_out else None
    return y_lat, y_ctx


def neighbourhood_mixer(h_lat, h_ctx, w_in, w_o, rpb, ctx_out):
    b, n, d = h_lat.shape
    rows = n // GRID_W
    kh = min(NAT_WIN_ROWS, rows)
    kw = NAT_WIN_COLS
    nh, hd = NAT_HEADS, NAT_HEAD_DIM
    scale = hd ** -0.5

    p = h_lat @ w_in
    q = p[..., :d].reshape(b, rows, GRID_W, nh, hd) * scale
    k = p[..., d:2 * d].reshape(b, rows, GRID_W, nh, hd)
    v = p[..., 2 * d:].reshape(b, rows, GRID_W, nh, hd)

    pc = h_ctx @ (w_in if ctx_out else w_in[:, d:])
    pc = pc[..., d:] if ctx_out else pc
    L = h_ctx.shape[1]
    k_c = pc[..., :d].reshape(b, L, nh, hd)
    v_c = pc[..., d:].reshape(b, L, nh, hd)

    row_start = jnp.clip(jnp.arange(rows) - kh // 2, 0, rows - kh)
    cols = jnp.arange(GRID_W)
    col_start = jnp.clip(cols - kw // 2, 0, GRID_W - kw)
    col_in = (cols[None, :] >= col_start[:, None]) & (cols[None, :] < col_start[:, None] + kw)
    dc_idx = jnp.clip(cols[None, :] - cols[:, None] + NAT_WIN_COLS - 1, 0, 2 * NAT_WIN_COLS - 2)
    rpb_f = rpb.astype(jnp.float32)
    band = kh * GRID_W

    def row_block(args):
        q_r, r = args
        start = row_start[r]
        k_band = lax.dynamic_slice_in_dim(k, start, kh, axis=1)
        v_band = lax.dynamic_slice_in_dim(v, start, kh, axis=1)
        dr_idx = start + jnp.arange(kh) - r + NAT_WIN_ROWS - 1
        bias = rpb_f[:, dr_idx[None, :, None], dc_idx[:, None, :]]
        bias = jnp.where(col_in[:, None, :], bias, -jnp.inf)
        s_loc = jnp.einsum('bqhd,bikhd->bhqik', q_r, k_band).astype(jnp.float32) + bias
        s_ctx = jnp.einsum('bqhd,blhd->bhql', q_r, k_c).astype(jnp.float32)
        s = jnp.concatenate([s_loc.reshape(b, nh, GRID_W, band), s_ctx], axis=-1)
        prob = jax.nn.softmax(s, axis=-1).astype(v.dtype)
        p_loc = prob[..., :band].reshape(b, nh, GRID_W, kh, GRID_W)
        return (jnp.einsum('bhqik,bikhd->bqhd', p_loc, v_band)
                + jnp.einsum('bhql,blhd->bqhd', prob[..., band:], v_c))

    o = lax.map(row_block, (jnp.moveaxis(q, 1, 0), jnp.arange(rows)))
    y_lat = jnp.moveaxis(o, 0, 1).reshape(b, n, d) @ w_o

    y_ctx = None
    if ctx_out:
        q_c = pc_q = (h_ctx @ w_in[:, :d]).reshape(b, L, nh, hd) * scale
        s = jnp.einsum('blhd,bmhd->bhlm', q_c, k_c).astype(jnp.float32)
        prob = jax.nn.softmax(s, axis=-1).astype(v_c.dtype)
        y_ctx = jnp.einsum('bhlm,bmhd->blhd', prob, v_c).reshape(b, L, d) @ w_o
    return y_lat, y_ctx


def sq_relu_mlp(h, w1, w2):
    return jnp.square(jax.nn.relu(h @ w1)) @ w2


def setup_inputs(seed: int = 0) -> dict:
    key = jax.random.key(seed)
    ks = jax.random.split(key, 17)
    f32 = jnp.float32

    def nrm(k, shape, scale):
        return jax.random.normal(k, shape, f32) * scale

    qk_w = RET_HEADS * RET_QK_DIM
    v_w = RET_HEADS * RET_V_DIM
    base_decay = jnp.log(-jnp.log(1.0 - 2.0 ** (-5.0 - jnp.arange(RET_HEADS, dtype=f32))))
    return {
        'x': nrm(ks[0], (BATCH, SEQ, D_MODEL), 1.0),
        'c': nrm(ks[1], (BATCH, D_MODEL), 1.0),
        'ctx': nrm(ks[2], (BATCH, CTX_LEN, D_MODEL), 1.0),
        'c_ctx': nrm(ks[3], (D_MODEL,), 1.0),
        'ada_w': nrm(ks[4], (DEPTH, D_MODEL, 6 * D_MODEL), 0.5 * D_MODEL ** -0.5),
        'ada_b': nrm(ks[5], (DEPTH, 6 * D_MODEL), 0.02),
        'ret_w_in': nrm(ks[6], (N_RET_LAYERS, D_MODEL, 2 * qk_w + 2 * v_w), D_MODEL ** -0.5),
        'ret_w_o': nrm(ks[7], (N_RET_LAYERS, v_w, D_MODEL), DN_BETA * v_w ** -0.5),
        'ret_decay': base_decay + nrm(ks[8], (N_RET_LAYERS, 2, RET_HEADS), 0.1),
        'nat_w_in': nrm(ks[9], (N_NAT_LAYERS, D_MODEL, 3 * D_MODEL), D_MODEL ** -0.5),
        'nat_w_o': nrm(ks[10], (N_NAT_LAYERS, D_MODEL, D_MODEL), DN_BETA * D_MODEL ** -0.5),
        'nat_rpb': nrm(ks[11], (N_NAT_LAYERS, NAT_HEADS, 2 * NAT_WIN_ROWS - 1, 2 * NAT_WIN_COLS - 1), 0.1),
        'mlp_w1': nrm(ks[12], (DEPTH, D_MODEL, MLP_HIDDEN), D_MODEL ** -0.5),
        'mlp_w2': nrm(ks[13], (DEPTH, MLP_HIDDEN, D_MODEL), DN_BETA * MLP_HIDDEN ** -0.5),
        'ln_g': 1.0 + nrm(ks[14], (DEPTH, 2, D_MODEL), 0.02),
        'ln_b': nrm(ks[15], (DEPTH, 2, D_MODEL), 0.02),
    }


def reference(x, c, ctx, c_ctx, ada_w, ada_b, ret_w_in, ret_w_o, ret_decay,
              nat_w_in, nat_w_o, nat_rpb, mlp_w1, mlp_w2, ln_g, ln_b):
    n = x.shape[1]
    ang_row, ang_col = axial_rope_angles(n, RET_QK_DIM // 2)
    silu_c = jax.nn.silu(c)
    silu_cc = jax.nn.silu(c_ctx)
    for i in range(DEPTH):
        last = i == DEPTH - 1
        j = i // N_MIXERS
        mod = (silu_c @ ada_w[i] + ada_b[i])[:, None, :]
        mod_c = silu_cc @ ada_w[i] + ada_b[i]
        sh1, sc1, g1, sh2, sc2, g2 = jnp.split(mod, 6, axis=-1)
        sh1c, sc1c, g1c, sh2c, sc2c, g2c = jnp.split(mod_c, 6, axis=-1)

        h = x * (1.0 + sc1) + sh1
        hc = ctx * (1.0 + sc1c) + sh1c
        if i % N_MIXERS == 0:
            y, yc = retention_mixer(h, hc, ret_w_in[j], ret_w_o[j], ret_decay[j],
                                    ang_row, ang_col, not last)
        else:
            y, yc = neighbourhood_mixer(h, hc, nat_w_in[j], nat_w_o[j], nat_rpb[j], not last)

        x = layer_norm(DN_ALPHA * x + g1 * y, ln_g[i, 0], ln_b[i, 0])
        h = x * (1.0 + sc2) + sh2
        x = layer_norm(DN_ALPHA * x + g2 * sq_relu_mlp(h, mlp_w1[i], mlp_w2[i]), ln_g[i, 1], ln_b[i, 1])

        if not last:
            ctx = layer_norm(DN_ALPHA * ctx + g1c * yc, ln_g[i, 0], ln_b[i, 0])
            hc = ctx * (1.0 + sc2c) + sh2c
            ctx = layer_norm(DN_ALPHA * ctx + g2c * sq_relu_mlp(hc, mlp_w1[i], mlp_w2[i]),
                             ln_g[i, 1], ln_b[i, 1])
    return x
```

```python
import functools

import jax
import jax.numpy as jnp
import numpy as np
from jax import lax
from jax.experimental import pallas as pl
from jax.experimental.pallas import tpu as pltpu

F32 = jnp.float32
BF16 = jnp.bfloat16

DEPTH = 4
GRID_W = 64
N_MIXERS = 2
RET_HEADS = 4
RET_CHUNK = 128
NAT_HEAD_DIM = 32
NAT_WIN_ROWS = 8
NAT_WIN_COLS = 16
ROPE_BASE = 10000.0
LN_EPS = 1e-5
DN_ALPHA = (2.0 * DEPTH) ** 0.25

LANES = 128
MXU_DIM = 256
VMEM_LIMIT = 56 * 1024 * 1024
MOD_ROWS = 16


def _params(n_axes, vmem=VMEM_LIMIT):
    return pltpu.CompilerParams(
        dimension_semantics=("arbitrary",) * n_axes, vmem_limit_bytes=vmem)


def _silu(x):
    return x / (1.0 + jnp.exp(-x))


def _layer_norm(z, g, b):
    mu = jnp.mean(z, axis=-1, keepdims=True)
    zc = z - mu
    var = jnp.mean(zc * zc, axis=-1, keepdims=True)
    return zc * lax.rsqrt(var + LN_EPS) * g + b


def _dot(a, b):
    return jnp.dot(a, b, preferred_element_type=F32)


def _dot_nt(a, b):
    return lax.dot_general(a, b, (((1,), (1,)), ((), ())), preferred_element_type=F32)


def _ada_kernel(c_ref, w_ref, b_ref, o_ref):
    s = _silu(c_ref[...])
    o_ref[0] = jnp.dot(s, w_ref[0], preferred_element_type=F32,
                       precision=lax.Precision.HIGHEST) + b_ref[0]


def _ada_mods(c, c_ctx, ada_w, ada_b):
    depth, d, d6 = ada_w.shape
    b = c.shape[0]
    cc = jnp.zeros((MOD_ROWS, d), F32).at[:b].set(c).at[b].set(c_ctx)
    tn = d6 // 4
    out = pl.pallas_call(
        _ada_kernel,
        grid=(depth, d6 // tn),
        in_specs=[
            pl.BlockSpec((MOD_ROWS, d), lambda i, j: (0, 0)),
            pl.BlockSpec((1, d, tn), lambda i, j: (i, 0, j)),
            pl.BlockSpec((1, 1, tn), lambda i, j: (i, 0, j)),
        ],
        out_specs=pl.BlockSpec((1, MOD_ROWS, tn), lambda i, j: (i, 0, j)),
        out_shape=jax.ShapeDtypeStruct((depth, MOD_ROWS, d6), F32),
        compiler_params=_params(2),
        name="ada_mods",
    )(cc, ada_w, ada_b.reshape(depth, 1, d6))
    return out.reshape(depth, MOD_ROWS, 6, d)


def _proj_kernel(x_ref, mod_ref, w_ref, cs_ref, *rest, rope_tiles, tn):
    if rope_tiles:
        cos_ref, sin_ref, o_ref, h_scr = rest
    else:
        o_ref, h_scr = rest
    j = pl.program_id(2)

    @pl.when(j == 0)
    def _():
        shift = mod_ref[0, 0:1, :]
        scale = mod_ref[0, 1:2, :]
        h_scr[...] = (x_ref[0] * (1.0 + scale) + shift).astype(BF16)

    acc = _dot(h_scr[...], w_ref[...]) * cs_ref[...]
    if rope_tiles:
        @pl.when(j < rope_tiles)
        def _():
            for s in range(tn // LANES):
                seg = acc[:, s * LANES:(s + 1) * LANES]
                t = (s % 2) * LANES
                cos = cos_ref[:, t:t + LANES]
                sin = sin_ref[:, t:t + LANES]
                rot = seg * cos + pltpu.roll(seg, LANES // 2, 1) * sin
                o_ref[0, :, s * LANES:(s + 1) * LANES] = rot.astype(BF16)

        @pl.when(j >= rope_tiles)
        def _():
            o_ref[0] = acc.astype(BF16)
    else:
        o_ref[0] = acc.astype(BF16)


def _project(x, mods, mod_row, w, col_scale, rope=None, rope_cols=0, tm=1024, tn=1024):
    b, n, d = x.shape
    nout = w.shape[1]
    tm = min(tm, n)
    assert n % tm == 0 and nout % tn == 0 and rope_cols % tn == 0
    rope_tiles = rope_cols // tn
    if mod_row is None:
        mod_map = lambda bi, t, j: (bi, 0, 0)
    else:
        mod_map = lambda bi, t, j: (mod_row, 0, 0)
    in_specs = [
        pl.BlockSpec((1, tm, d), lambda bi, t, j: (bi, t, 0)),
        pl.BlockSpec((1, 6, d), mod_map),
        pl.BlockSpec((d, tn), lambda bi, t, j: (0, j)),
        pl.BlockSpec((1, tn), lambda bi, t, j: (0, j)),
    ]
    args = [x, mods, w, col_scale]
    if rope_tiles:
        cos_t, sin_t = rope
        hw = cos_t.shape[1]
        in_specs += [pl.BlockSpec((tm, hw), lambda bi, t, j: (t, 0))] * 2
        args += [cos_t, sin_t]
    return pl.pallas_call(
        functools.partial(_proj_kernel, rope_tiles=rope_tiles, tn=tn),
        grid=(b, n // tm, nout // tn),
        in_specs=in_specs,
        out_specs=pl.BlockSpec((1, tm, tn), lambda bi, t, j: (bi, t, j)),
        out_shape=jax.ShapeDtypeStruct((b, n, nout), BF16),
        scratch_shapes=[pltpu.VMEM((tm, d), BF16)],
        compiler_params=_params(3),
        name="proj",
    )(*args)


def _ret_kernel(dec_ref, qc_ref, kc_ref, vc_ref, ql_ref, kl_ref, vl_ref,
                ol_ref, oc_ref, of_scr, st_scr, intra_scr, qd_scr, kd_scr, *, chunk):
    cs = chunk
    n_ctx = qc_ref.shape[1]
    n_lat = ql_ref.shape[1]
    row = lax.broadcasted_iota(jnp.int32, (cs, cs), 0).astype(F32)
    col = lax.broadcasted_iota(jnp.int32, (cs, cs), 1).astype(F32)
    pos = lax.broadcasted_iota(jnp.int32, (cs, 1), 0).astype(F32)
    cdec = []
    for d in range(2):
        lg = -jnp.exp(dec_ref[0, d, 0:1, 0:1])
        rel = (row - col) if d == 0 else (col - row)
        intra_scr[d] = jnp.where(rel >= 0, jnp.exp(lg * jnp.maximum(rel, 0.0)), 0.0)
        if d == 0:
            qd_scr[d] = jnp.exp(lg * (pos + 1.0))
            kd_scr[d] = jnp.exp(lg * (cs - 1.0 - pos))
        else:
            qd_scr[d] = jnp.exp(lg * (cs - pos))
            kd_scr[d] = jnp.exp(lg * pos)
        cdec.append(jnp.exp(lg * cs))

    def chunk_out(d, q_ref, k_ref, v_ref, off):
        q = q_ref[0, pl.ds(off, cs), :]
        k = k_ref[0, pl.ds(off, cs), :]
        v = v_ref[0, pl.ds(off, cs), :]
        st = st_scr[...]
        s = _dot_nt(q, k) * intra_scr[d]
        o = _dot(s.astype(BF16), v) + qd_scr[d] * _dot(q, st.astype(BF16))
        kdt = (k.astype(F32) * kd_scr[d]).T.astype(BF16)
        st_scr[...] = st * cdec[d] + _dot(kdt, v)
        return o

    def normed(o):
        mu = jnp.mean(o, axis=-1, keepdims=True)
        oc = o - mu
        var = jnp.mean(oc * oc, axis=-1, keepdims=True)
        return (oc * lax.rsqrt(var + LN_EPS)).astype(BF16)

    st_scr[...] = jnp.zeros_like(st_scr)

    def fwd_ctx(i, carry):
        off = pl.multiple_of(i * cs, cs)
        of_scr[pl.ds(off, cs), :] = chunk_out(0, qc_ref, kc_ref, vc_ref, off)
        return carry

    def fwd_lat(i, carry):
        off = pl.multiple_of(i * cs, cs)
        dst = pl.multiple_of(n_ctx + i * cs, cs)
        of_scr[pl.ds(dst, cs), :] = chunk_out(0, ql_ref, kl_ref, vl_ref, off)
        return carry

    lax.fori_loop(0, n_ctx // cs, fwd_ctx, 0)
    lax.fori_loop(0, n_lat // cs, fwd_lat, 0)

    st_scr[...] = jnp.zeros_like(st_scr)

    def bwd_ctx(i, carry):
        off = pl.multiple_of((n_ctx // cs - 1 - i) * cs, cs)
        o = chunk_out(1, qc_ref, kc_ref, vc_ref, off)
        oc_ref[0, pl.ds(off, cs), :] = normed(of_scr[pl.ds(off, cs), :] + o)
        return carry

    def bwd_lat(i, carry):
        off = pl.multiple_of((n_lat // cs - 1 - i) * cs, cs)
        src = pl.multiple_of(n_ctx + (n_lat // cs - 1 - i) * cs, cs)
        o = chunk_out(1, ql_ref, kl_ref, vl_ref, off)
        ol_ref[0, pl.ds(off, cs), :] = normed(of_scr[pl.ds(src, cs), :] + o)
        return carry

    lax.fori_loop(0, n_ctx // cs, bwd_ctx, 0)
    lax.fori_loop(0, n_lat // cs, bwd_lat, 0)


def _retention(p_lat, p_ctx, decay, chunk=RET_CHUNK):
    b, n_lat, width = p_lat.shape
    n_ctx = p_ctx.shape[1]
    nh = RET_HEADS
    dk = width // (6 * nh)
    dv = 2 * dk
    assert n_ctx % chunk == 0 and n_lat % chunk == 0
    dec = jnp.broadcast_to(decay.astype(F32).T[:, :, None, None], (nh, 2, 8, LANES))
    kblk = (nh * dk) // dk
    vblk = (2 * nh * dk) // dv

    def specs(n):
        return [
            pl.BlockSpec((1, n, dk), lambda bi, h: (bi, 0, h)),
            pl.BlockSpec((1, n, dk), lambda bi, h: (bi, 0, kblk + h)),
            pl.BlockSpec((1, n, dv), lambda bi, h: (bi, 0, vblk + h)),
        ]

    return pl.pallas_call(
        functools.partial(_ret_kernel, chunk=chunk),
        grid=(b, nh),
        in_specs=[pl.BlockSpec((1, 2, 8, LANES), lambda bi, h: (h, 0, 0, 0))]
        + specs(n_ctx) + specs(n_lat),
        out_specs=[
            pl.BlockSpec((1, n_lat, dv), lambda bi, h: (bi, 0, h)),
            pl.BlockSpec((1, n_ctx, dv), lambda bi, h: (bi, 0, h)),
        ],
        out_shape=[
            jax.ShapeDtypeStruct((b, n_lat, nh * dv), BF16),
            jax.ShapeDtypeStruct((b, n_ctx, nh * dv), BF16),
        ],
        scratch_shapes=[
            pltpu.VMEM((n_ctx + n_lat, dv), F32),
            pltpu.VMEM((dk, dv), F32),
            pltpu.VMEM((2, chunk, chunk), F32),
            pltpu.VMEM((2, chunk, 1), F32),
            pltpu.VMEM((2, chunk, 1), F32),
        ],
        compiler_params=_params(2),
        name="retention",
    )(dec, p_ctx, p_ctx, p_ctx, p_lat, p_lat, p_lat)


HEADS_PER_GROUP = MXU_DIM // NAT_HEAD_DIM


def _group_attention(qg, keys, vals, biases):
    nq = qg.shape[0]
    lane_head = lax.broadcasted_iota(jnp.int32, (nq, MXU_DIM), 1) // NAT_HEAD_DIM
    zero = jnp.zeros_like(qg)
    qm = jnp.concatenate(
        [jnp.where(lane_head == h, qg, zero) for h in range(HEADS_PER_GROUP)], axis=0)
    scores = []
    for k, bias in zip(keys, biases):
        s = _dot_nt(qm, k)
        scores.append(s if bias is None else s + bias)
    m = scores[0].max(axis=-1, keepdims=True)
    for s in scores[1:]:
        m = jnp.maximum(m, s.max(axis=-1, keepdims=True))
    l = None
    r = None
    for s, v in zip(scores, vals):
        p = jnp.exp(s - m)
        ps = p.sum(axis=-1, keepdims=True)
        pv = _dot(p.astype(BF16), v)
        l = ps if l is None else l + ps
        r = pv if r is None else r + pv
    r = r * (1.0 / l)
    out = r[0:nq]
    for h in range(1, HEADS_PER_GROUP):
        out = jnp.where(lane_head == h, r[h * nq:(h + 1) * nq], out)
    return out


def _nat_kernel(q_ref, k_ref, v_ref, kc_ref, vc_ref, bias_ref, o_ref, *, rows_per_step, rows):
    kh = NAT_WIN_ROWS
    w = GRID_W
    n_groups = q_ref.shape[2] // MXU_DIM
    t = pl.program_id(1)

    def row_body(rr, carry):
        r = t * rows_per_step + rr
        start = jnp.clip(r - kh // 2, 0, rows - kh)
        koff = pl.multiple_of(start * w, w)
        qoff = pl.multiple_of(rr * w, w)
        dr0 = start - r + NAT_WIN_ROWS - 1
        for g in range(n_groups):
            lanes = slice(g * MXU_DIM, (g + 1) * MXU_DIM)
            qg = q_ref[0, pl.ds(qoff, w), lanes]
            kb = k_ref[0, pl.ds(koff, kh * w), lanes]
            vb = v_ref[0, pl.ds(koff, kh * w), lanes]
            bias = jnp.concatenate([
                jnp.concatenate([bias_ref[g * HEADS_PER_GROUP + h, dr0 + i]
                                 for i in range(0, kh, 2)], axis=1)
                for h in range(HEADS_PER_GROUP)], axis=0)
            out = _group_attention(qg, [kb, kc_ref[0, :, lanes]], [vb, vc_ref[0, :, lanes]],
                                   [bias, None])
            o_ref[0, pl.ds(qoff, w), lanes] = out.astype(BF16)
        return carry

    lax.fori_loop(0, rows_per_step, row_body, 0)


def _nat_bias_table(rpb):
    kw = NAT_WIN_COLS
    cols = jnp.arange(GRID_W)
    col_start = jnp.clip(cols - kw // 2, 0, GRID_W - kw)
    col_in = (cols[None, :] >= col_start[:, None]) & (cols[None, :] < col_start[:, None] + kw)
    dc_idx = jnp.clip(cols[None, :] - cols[:, None] + kw - 1, 0, 2 * kw - 2)
    full = rpb.astype(F32)[:, :, dc_idx]
    full = jnp.where(col_in[None, None], full, -jnp.inf)
    return jnp.concatenate([full[:, :-1], full[:, 1:]], axis=-1)


def _nat_attention(p_lat, p_ctx, bias2, rows_per_step=8):
    b, n, d3 = p_lat.shape
    d = d3 // 3
    n_ctx = p_ctx.shape[1]
    rows = n // GRID_W
    assert rows % rows_per_step == 0 and rows >= NAT_WIN_ROWS
    tq = rows_per_step * GRID_W
    return pl.pallas_call(
        functools.partial(_nat_kernel, rows_per_step=rows_per_step, rows=rows),
        grid=(b, rows // rows_per_step),
        in_specs=[
            pl.BlockSpec((1, tq, d), lambda bi, t: (bi, t, 0)),
            pl.BlockSpec((1, n, d), lambda bi, t: (bi, 0, 1), pipeline_mode=pl.Buffered(1)),
            pl.BlockSpec((1, n, d), lambda bi, t: (bi, 0, 2), pipeline_mode=pl.Buffered(1)),
            pl.BlockSpec((1, n_ctx, d), lambda bi, t: (bi, 0, 1)),
            pl.BlockSpec((1, n_ctx, d), lambda bi, t: (bi, 0, 2)),
            pl.BlockSpec(bias2.shape, lambda bi, t: (0, 0, 0, 0)),
        ],
        out_specs=pl.BlockSpec((1, tq, d), lambda bi, t: (bi, t, 0)),
        out_shape=jax.ShapeDtypeStruct((b, n, d), BF16),
        compiler_params=_params(2),
        name="nat_attention",
    )(p_lat, p_lat, p_lat, p_ctx, p_ctx, bias2)


def _ctx_attn_kernel(q_ref, k_ref, v_ref, o_ref, *, tq):
    n_groups = q_ref.shape[2] // MXU_DIM
    for qb in range(q_ref.shape[1] // tq):
        for g in range(n_groups):
            lanes = slice(g * MXU_DIM, (g + 1) * MXU_DIM)
            out = _group_attention(q_ref[0, qb * tq:(qb + 1) * tq, lanes],
                                   [k_ref[0, :, lanes]], [v_ref[0, :, lanes]], [None])
            o_ref[0, qb * tq:(qb + 1) * tq, lanes] = out.astype(BF16)


def _ctx_attention(p_ctx, tq=64):
    b, n_ctx, d3 = p_ctx.shape
    d = d3 // 3
    return pl.pallas_call(
        functools.partial(_ctx_attn_kernel, tq=tq),
        grid=(b,),
        in_specs=[pl.BlockSpec((1, n_ctx, d), lambda bi, c=c: (bi, 0, c)) for c in range(3)],
        out_specs=pl.BlockSpec((1, n_ctx, d), lambda bi: (bi, 0, 0)),
        out_shape=jax.ShapeDtypeStruct((b, n_ctx, d), BF16),
        compiler_params=_params(1),
        name="ctx_attention",
    )(p_ctx, p_ctx, p_ctx)


def _post_kernel(a_ref, *rest, gated, mlp_chunk):
    if gated:
        g_ref, x_ref, mod_ref, wo_ref, w1_ref, w2_ref, ln_ref, o_ref = rest
    else:
        x_ref, mod_ref, wo_ref, w1_ref, w2_ref, ln_ref, o_ref = rest
    a = a_ref[0]
    if gated:
        a = (a.astype(F32) * _silu(g_ref[0].astype(F32))).astype(BF16)
    y = _dot(a, wo_ref[...])
    gate1 = mod_ref[0, 2:3, :]
    shift2 = mod_ref[0, 3:4, :]
    scale2 = mod_ref[0, 4:5, :]
    gate2 = mod_ref[0, 5:6, :]
    x1 = _layer_norm(DN_ALPHA * x_ref[0] + gate1 * y, ln_ref[0:1, :], ln_ref[1:2, :])
    h = (x1 * (1.0 + scale2) + shift2).astype(BF16)
    hidden = w1_ref.shape[1]
    acc = None
    for c in range(hidden // mlp_chunk):
        cols = slice(c * mlp_chunk, (c + 1) * mlp_chunk)
        u = jnp.maximum(_dot(h, w1_ref[:, cols]), 0.0)
        part = _dot((u * u).astype(BF16), w2_ref[cols, :])
        acc = part if acc is None else acc + part
    o_ref[0] = _layer_norm(DN_ALPHA * x1 + gate2 * acc, ln_ref[2:3, :], ln_ref[3:4, :])


def _post(a, gate_src, gate_blk, x, mods, mod_row, wo, w1, w2, ln, tm=512, mlp_chunk=1024):
    b, n, d = x.shape
    ka = a.shape[2]
    tm = min(tm, n)
    assert n % tm == 0
    gated = gate_src is not None
    if mod_row is None:
        mod_map = lambda bi, t: (bi, 0, 0)
    else:
        mod_map = lambda bi, t: (mod_row, 0, 0)
    in_specs = [pl.BlockSpec((1, tm, ka), lambda bi, t: (bi, t, 0))]
    args = [a]
    if gated:
        in_specs.append(pl.BlockSpec((1, tm, ka), lambda bi, t: (bi, t, gate_blk)))
        args.append(gate_src)
    in_specs += [
        pl.BlockSpec((1, tm, d), lambda bi, t: (bi, t, 0)),
        pl.BlockSpec((1, 6, d), mod_map),
        pl.BlockSpec(wo.shape, lambda bi, t: (0, 0)),
        pl.BlockSpec(w1.shape, lambda bi, t: (0, 0)),
        pl.BlockSpec(w2.shape, lambda bi, t: (0, 0)),
        pl.BlockSpec(ln.shape, lambda bi, t: (0, 0)),
    ]
    args += [x, mods, wo, w1, w2, ln]
    return pl.pallas_call(
        functools.partial(_post_kernel, gated=gated, mlp_chunk=mlp_chunk),
        grid=(b, n // tm),
        in_specs=in_specs,
        out_specs=pl.BlockSpec((1, tm, d), lambda bi, t: (bi, t, 0)),
        out_shape=jax.ShapeDtypeStruct((b, n, d), F32),
        compiler_params=_params(2),
        name="post",
    )(*args)


def _rope_tables(n, head_dim):
    axis_dim = head_dim // 2
    t = jnp.arange(n)
    row = (t // GRID_W).astype(F32)
    col = (t % GRID_W).astype(F32)
    inv = 1.0 / (ROPE_BASE ** (jnp.arange(0, axis_dim, 2, dtype=F32) / axis_dim))
    ar = row[:, None] * inv
    ac = col[:, None] * inv
    cos = jnp.concatenate([jnp.cos(ar), jnp.cos(ar), jnp.cos(ac), jnp.cos(ac)], axis=-1)
    sin = jnp.concatenate([-jnp.sin(ar), jnp.sin(ar), -jnp.sin(ac), jnp.sin(ac)], axis=-1)
    return cos, sin


def kernel(x, c, ctx, c_ctx, ada_w, ada_b, ret_w_in, ret_w_o, ret_decay,
           nat_w_in, nat_w_o, nat_rpb, mlp_w1, mlp_w2, ln_g, ln_b):
    b, n, d = x.shape
    mods_all = _ada_mods(c, c_ctx, ada_w, ada_b)
    ctx_row = b

    qk_w = d
    dk = qk_w // RET_HEADS
    rope = _rope_tables(n, dk)
    ret_cs = jnp.concatenate([jnp.ones((qk_w,), F32), jnp.full((qk_w,), dk ** -0.5, F32),
                              jnp.ones((ret_w_in.shape[2] - 2 * qk_w,), F32)])[None, :]
    nat_cs = jnp.concatenate([jnp.full((d,), NAT_HEAD_DIM ** -0.5, F32),
                              jnp.ones((2 * d,), F32)])[None, :]

    for i in range(DEPTH):
        last = i == DEPTH - 1
        j = i // N_MIXERS
        mods = mods_all[i]
        ln = jnp.stack([ln_g[i, 0], ln_b[i, 0], ln_g[i, 1], ln_b[i, 1]])
        w1 = mlp_w1[i].astype(BF16)
        w2 = mlp_w2[i].astype(BF16)
        if i % N_MIXERS == 0:
            w_in = ret_w_in[j].astype(BF16)
            wo = ret_w_o[j].astype(BF16)
            p_lat = _project(x, mods, None, w_in, ret_cs, rope=rope, rope_cols=2 * qk_w)
            p_ctx = _project(ctx, mods, ctx_row, w_in, ret_cs)
            a_lat, a_ctx = _retention(p_lat, p_ctx, ret_decay[j])
            gate_blk = (p_lat.shape[2] // a_lat.shape[2]) - 1
            g_lat, g_ctx = p_lat, p_ctx
        else:
            w_in = nat_w_in[j].astype(BF16)
            wo = nat_w_o[j].astype(BF16)
            p_lat = _project(x, mods, None, w_in, nat_cs)
            p_ctx = _project(ctx, mods, ctx_row, w_in, nat_cs)
            a_lat = _nat_attention(p_lat, p_ctx, _nat_bias_table(nat_rpb[j]))
            a_ctx = None if last else _ctx_attention(p_ctx)
            gate_blk = 0
            g_lat = g_ctx = None
        x = _post(a_lat, g_lat, gate_blk, x, mods, None, wo, w1, w2, ln)
        if not last:
            ctx = _post(a_ctx, g_ctx, gate_blk, ctx, mods, ctx_row, wo, w1, w2, ln)
    return x
```

```python
import functools

import jax
import jax.numpy as jnp
import numpy as np
from jax import lax
from jax.experimental import pallas as pl
from jax.experimental.pallas import tpu as pltpu

F32 = jnp.float32
BF16 = jnp.bfloat16

DEPTH = 4
GRID_W = 64
N_MIXERS = 2
RET_HEADS = 4
RET_CHUNK = 256
NAT_HEAD_DIM = 32
NAT_WIN_ROWS = 8
NAT_WIN_COLS = 16
ROPE_BASE = 10000.0
LN_EPS = 1e-5
DN_ALPHA = (2.0 * DEPTH) ** 0.25
LOG2E = 1.4426950408889634

LANES = 128
MXU_DIM = 256
VMEM_LIMIT = 56 * 1024 * 1024
MOD_ROWS = 16


def _params(n_axes, vmem=VMEM_LIMIT):
    return pltpu.CompilerParams(
        dimension_semantics=("arbitrary",) * n_axes, vmem_limit_bytes=vmem)


def _silu(x):
    return x / (1.0 + jnp.exp(-x))


def _layer_norm(z, g, b):
    mu = jnp.mean(z, axis=-1, keepdims=True)
    zc = z - mu
    var = jnp.mean(zc * zc, axis=-1, keepdims=True)
    return zc * lax.rsqrt(var + LN_EPS) * g + b


def _dot(a, b):
    return jnp.dot(a, b, preferred_element_type=F32)


def _dot_nt(a, b):
    return lax.dot_general(a, b, (((1,), (1,)), ((), ())), preferred_element_type=F32)


def _ada_kernel(c_ref, w_ref, b_ref, o_ref):
    s = _silu(c_ref[...])
    o_ref[0] = jnp.dot(s, w_ref[0], preferred_element_type=F32,
                       precision=lax.Precision.HIGHEST) + b_ref[0]


def _ada_mods(c, c_ctx, ada_w, ada_b):
    depth, d, d6 = ada_w.shape
    b = c.shape[0]
    cc = jnp.zeros((MOD_ROWS, d), F32).at[:b].set(c).at[b].set(c_ctx)
    tn = d6 // 4
    out = pl.pallas_call(
        _ada_kernel,
        grid=(depth, d6 // tn),
        in_specs=[
            pl.BlockSpec((MOD_ROWS, d), lambda i, j: (0, 0)),
            pl.BlockSpec((1, d, tn), lambda i, j: (i, 0, j)),
            pl.BlockSpec((1, 1, tn), lambda i, j: (i, 0, j)),
        ],
        out_specs=pl.BlockSpec((1, MOD_ROWS, tn), lambda i, j: (i, 0, j)),
        out_shape=jax.ShapeDtypeStruct((depth, MOD_ROWS, d6), F32),
        compiler_params=_params(2),
        name="ada_mods",
    )(cc, ada_w, ada_b.reshape(depth, 1, d6))
    return out.reshape(depth, MOD_ROWS, 6, d)


def _proj_kernel(x_ref, mod_ref, w_ref, cs_ref, *rest, rope, tn):
    if rope:
        cos_ref, sin_ref, o_ref, h_scr = rest
    else:
        o_ref, h_scr = rest

    @pl.when(pl.program_id(2) == 0)
    def _():
        shift = mod_ref[0, 0:1, :]
        scale = mod_ref[0, 1:2, :]
        h_scr[...] = (x_ref[0] * (1.0 + scale) + shift).astype(BF16)

    acc = _dot(h_scr[...], w_ref[...]) * cs_ref[...]
    if rope:
        for s in range(tn // LANES):
            seg = acc[:, s * LANES:(s + 1) * LANES]
            t = (s % 2) * LANES
            cos = cos_ref[:, t:t + LANES]
            sin = sin_ref[:, t:t + LANES]
            rot = seg * cos + pltpu.roll(seg, LANES // 2, 1) * sin
            o_ref[0, :, s * LANES:(s + 1) * LANES] = rot.astype(BF16)
    else:
        o_ref[0] = acc.astype(BF16)


def _project(x, mods, mod_row, w, col_scale, rope=None, tm=1024, tn=1024):
    b, n, d = x.shape
    nout = w.shape[1]
    tm = min(tm, n)
    assert n % tm == 0 and nout % tn == 0
    if mod_row is None:
        mod_map = lambda bi, t, j: (bi, 0, 0)
    else:
        mod_map = lambda bi, t, j: (mod_row, 0, 0)
    in_specs = [
        pl.BlockSpec((1, tm, d), lambda bi, t, j: (bi, t, 0)),
        pl.BlockSpec((1, 6, d), mod_map),
        pl.BlockSpec((d, tn), lambda bi, t, j: (0, j)),
        pl.BlockSpec((1, tn), lambda bi, t, j: (0, j)),
    ]
    args = [x, mods, w, col_scale]
    if rope is not None:
        cos_t, sin_t = rope
        hw = cos_t.shape[1]
        assert tn % hw == 0
        in_specs += [pl.BlockSpec((tm, hw), lambda bi, t, j: (t, 0))] * 2
        args += [cos_t, sin_t]
    return pl.pallas_call(
        functools.partial(_proj_kernel, rope=rope is not None, tn=tn),
        grid=(b, n // tm, nout // tn),
        in_specs=in_specs,
        out_specs=pl.BlockSpec((1, tm, tn), lambda bi, t, j: (bi, t, j)),
        out_shape=jax.ShapeDtypeStruct((b, n, nout), BF16),
        scratch_shapes=[pltpu.VMEM((tm, d), BF16)],
        compiler_params=_params(3),
        name="proj",
    )(*args)


def _ret_kernel(dec_ref, qc_ref, kc_ref, vc_ref, ql_ref, kl_ref, vl_ref,
                ol_ref, oc_ref, raw_scr, st_scr, intra_scr, qd_scr, kd_scr, *, chunk):
    cs = chunk
    row = lax.broadcasted_iota(jnp.int32, (cs, cs), 0).astype(F32)
    col = lax.broadcasted_iota(jnp.int32, (cs, cs), 1).astype(F32)
    pos = lax.broadcasted_iota(jnp.int32, (cs, 1), 0).astype(F32)
    cdec = []
    for d in range(2):
        lg = -jnp.exp(dec_ref[0, d, 0:1, 0:1])
        rel = (row - col) if d == 0 else (col - row)
        intra_scr[d] = jnp.where(rel >= 0, jnp.exp(lg * jnp.maximum(rel, 0.0)), 0.0)
        if d == 0:
            qd_scr[d] = jnp.exp(lg * (pos + 1.0))
            kd_scr[d] = jnp.exp(lg * (cs - 1.0 - pos))
        else:
            qd_scr[d] = jnp.exp(lg * (cs - pos))
            kd_scr[d] = jnp.exp(lg * pos)
        cdec.append(jnp.exp(lg * cs))
    st_scr[...] = jnp.zeros_like(st_scr)

    def normed(o):
        mu = jnp.mean(o, axis=-1, keepdims=True)
        oc = o - mu
        var = jnp.mean(oc * oc, axis=-1, keepdims=True)
        return (oc * lax.rsqrt(var + LN_EPS)).astype(BF16)

    def segment(q_ref, k_ref, v_ref, o_ref):
        nch = q_ref.shape[1] // cs

        def chunk_out(d, off):
            q = q_ref[0, pl.ds(off, cs), :]
            k = k_ref[0, pl.ds(off, cs), :]
            v = v_ref[0, pl.ds(off, cs), :]
            st = st_scr[d]
            s = _dot_nt(q, k) * intra_scr[d]
            o = _dot(s.astype(BF16), v) + qd_scr[d] * _dot(q, st.astype(BF16))
            kdt = (k.astype(F32) * kd_scr[d]).T.astype(BF16)
            st_scr[d] = st * cdec[d] + _dot(kdt, v)
            return o

        if nch == 1:
            o_ref[0] = normed(chunk_out(0, 0) + chunk_out(1, 0))
            return
        assert nch % 2 == 0

        def offsets(i):
            return pl.multiple_of(i * cs, cs), pl.multiple_of((nch - 1 - i) * cs, cs)

        def first_half(i, carry):
            off_f, off_b = offsets(i)
            raw_scr[pl.ds(off_f, cs), :] = chunk_out(0, off_f)
            raw_scr[pl.ds(off_b, cs), :] = chunk_out(1, off_b)
            return carry

        def second_half(i, carry):
            off_f, off_b = offsets(i)
            o_ref[0, pl.ds(off_f, cs), :] = normed(chunk_out(0, off_f) + raw_scr[pl.ds(off_f, cs), :])
            o_ref[0, pl.ds(off_b, cs), :] = normed(chunk_out(1, off_b) + raw_scr[pl.ds(off_b, cs), :])
            return carry

        lax.fori_loop(0, nch // 2, first_half, 0)
        lax.fori_loop(nch // 2, nch, second_half, 0)

    segment(qc_ref, kc_ref, vc_ref, oc_ref)
    segment(ql_ref, kl_ref, vl_ref, ol_ref)


def _retention(qk_lat, vg_lat, qk_ctx, vg_ctx, decay, chunk=RET_CHUNK):
    b, n_lat, width = qk_lat.shape
    n_ctx = qk_ctx.shape[1]
    nh = RET_HEADS
    dk = width // (2 * nh)
    dv = vg_lat.shape[2] // (2 * nh)
    assert n_ctx % chunk == 0 and n_lat % chunk == 0
    dec = jnp.broadcast_to(decay.astype(F32).T[:, :, None, None], (nh, 2, 8, LANES))

    def specs(n):
        return [
            pl.BlockSpec((1, n, dk), lambda bi, h: (bi, 0, h)),
            pl.BlockSpec((1, n, dk), lambda bi, h: (bi, 0, nh + h)),
            pl.BlockSpec((1, n, dv), lambda bi, h: (bi, 0, h)),
        ]

    return pl.pallas_call(
        functools.partial(_ret_kernel, chunk=chunk),
        grid=(b, nh),
        in_specs=[pl.BlockSpec((1, 2, 8, LANES), lambda bi, h: (h, 0, 0, 0))]
        + specs(n_ctx) + specs(n_lat),
        out_specs=[
            pl.BlockSpec((1, n_lat, dv), lambda bi, h: (bi, 0, h)),
            pl.BlockSpec((1, n_ctx, dv), lambda bi, h: (bi, 0, h)),
        ],
        out_shape=[
            jax.ShapeDtypeStruct((b, n_lat, nh * dv), BF16),
            jax.ShapeDtypeStruct((b, n_ctx, nh * dv), BF16),
        ],
        scratch_shapes=[
            pltpu.VMEM((max(n_ctx, n_lat), dv), F32),
            pltpu.VMEM((2, dk, dv), F32),
            pltpu.VMEM((2, chunk, chunk), F32),
            pltpu.VMEM((2, chunk, 1), F32),
            pltpu.VMEM((2, chunk, 1), F32),
        ],
        compiler_params=_params(2),
        name="retention",
    )(dec, qk_ctx, qk_ctx, vg_ctx, qk_lat, qk_lat, vg_lat)


HEADS_PER_GROUP = MXU_DIM // NAT_HEAD_DIM


def _group_attention(qg, keys, vals, biases):
    nq = qg.shape[0]
    lane_head = lax.broadcasted_iota(jnp.int32, (nq, MXU_DIM), 1) // NAT_HEAD_DIM
    zero = jnp.zeros_like(qg)
    qm = jnp.concatenate(
        [jnp.where(lane_head == h, qg, zero) for h in range(HEADS_PER_GROUP)], axis=0)
    scores = []
    for k, bias in zip(keys, biases):
        s = _dot_nt(qm, k)
        scores.append(s if bias is None else s + bias)

    def lane_fold(parts, op):
        acc = None
        for a in parts:
            for c in range(0, a.shape[1], MXU_DIM):
                blk = a[:, c:c + MXU_DIM]
                acc = blk if acc is None else op(acc, blk)
        return acc

    m = lane_fold(scores, jnp.maximum).max(axis=-1, keepdims=True)
    probs = [jnp.exp2(s - m) for s in scores]
    l = lane_fold(probs, jnp.add).sum(axis=-1, keepdims=True)
    r = None
    for p, v in zip(probs, vals):
        pv = _dot(p.astype(BF16), v)
        r = pv if r is None else r + pv
    r = r * (1.0 / l)
    out = r[0:nq]
    for h in range(1, HEADS_PER_GROUP):
        out = jnp.where(lane_head == h, r[h * nq:(h + 1) * nq], out)
    return out


def _nat_kernel(q_ref, k_ref, v_ref, kc_ref, vc_ref, bias_ref, o_ref, *, rows_per_step, rows):
    kh = NAT_WIN_ROWS
    w = GRID_W
    n_groups = q_ref.shape[2] // MXU_DIM
    t = pl.program_id(1)

    def row_body(rr, carry):
        r = t * rows_per_step + rr
        start = jnp.clip(r - kh // 2, 0, rows - kh)
        koff = pl.multiple_of(start * w, w)
        qoff = pl.multiple_of(rr * w, w)
        dr0 = start - r + NAT_WIN_ROWS - 1
        for g in range(n_groups):
            lanes = slice(g * MXU_DIM, (g + 1) * MXU_DIM)
            qg = q_ref[0, pl.ds(qoff, w), lanes]
            kb = k_ref[0, pl.ds(koff, kh * w), lanes]
            vb = v_ref[0, pl.ds(koff, kh * w), lanes]
            bias = jnp.concatenate([
                jnp.concatenate([bias_ref[g * HEADS_PER_GROUP + h, dr0 + i]
                                 for i in range(0, kh, 2)], axis=1)
                for h in range(HEADS_PER_GROUP)], axis=0)
            out = _group_attention(qg, [kb, kc_ref[0, :, lanes]], [vb, vc_ref[0, :, lanes]],
                                   [bias, None])
            o_ref[0, pl.ds(qoff, w), lanes] = out.astype(BF16)
        return carry

    lax.fori_loop(0, rows_per_step, row_body, 0)


def _nat_bias_table(rpb):
    kw = NAT_WIN_COLS
    cols = jnp.arange(GRID_W)
    col_start = jnp.clip(cols - kw // 2, 0, GRID_W - kw)
    col_in = (cols[None, :] >= col_start[:, None]) & (cols[None, :] < col_start[:, None] + kw)
    dc_idx = jnp.clip(cols[None, :] - cols[:, None] + kw - 1, 0, 2 * kw - 2)
    full = rpb.astype(F32)[:, :, dc_idx]
    full = jnp.where(col_in[None, None], full * LOG2E, -jnp.inf)
    return jnp.concatenate([full[:, :-1], full[:, 1:]], axis=-1)


def _nat_attention(p_lat, p_ctx, bias2, rows_per_step=8):
    b, n, d3 = p_lat.shape
    d = d3 // 3
    n_ctx = p_ctx.shape[1]
    rows = n // GRID_W
    assert rows % rows_per_step == 0 and rows >= NAT_WIN_ROWS
    tq = rows_per_step * GRID_W
    return pl.pallas_call(
        functools.partial(_nat_kernel, rows_per_step=rows_per_step, rows=rows),
        grid=(b, rows // rows_per_step),
        in_specs=[
            pl.BlockSpec((1, tq, d), lambda bi, t: (bi, t, 0)),
            pl.BlockSpec((1, n, d), lambda bi, t: (bi, 0, 1), pipeline_mode=pl.Buffered(1)),
            pl.BlockSpec((1, n, d), lambda bi, t: (bi, 0, 2), pipeline_mode=pl.Buffered(1)),
            pl.BlockSpec((1, n_ctx, d), lambda bi, t: (bi, 0, 1)),
            pl.BlockSpec((1, n_ctx, d), lambda bi, t: (bi, 0, 2)),
            pl.BlockSpec(bias2.shape, lambda bi, t: (0, 0, 0, 0)),
        ],
        out_specs=pl.BlockSpec((1, tq, d), lambda bi, t: (bi, t, 0)),
        out_shape=jax.ShapeDtypeStruct((b, n, d), BF16),
        compiler_params=_params(2),
        name="nat_attention",
    )(p_lat, p_lat, p_lat, p_ctx, p_ctx, bias2)


def _ctx_attn_kernel(q_ref, k_ref, v_ref, o_ref, *, tq):
    n_groups = q_ref.shape[2] // MXU_DIM
    for qb in range(q_ref.shape[1] // tq):
        for g in range(n_groups):
            lanes = slice(g * MXU_DIM, (g + 1) * MXU_DIM)
            out = _group_attention(q_ref[0, qb * tq:(qb + 1) * tq, lanes],
                                   [k_ref[0, :, lanes]], [v_ref[0, :, lanes]], [None])
            o_ref[0, qb * tq:(qb + 1) * tq, lanes] = out.astype(BF16)


def _ctx_attention(p_ctx, tq=64):
    b, n_ctx, d3 = p_ctx.shape
    d = d3 // 3
    return pl.pallas_call(
        functools.partial(_ctx_attn_kernel, tq=tq),
        grid=(b,),
        in_specs=[pl.BlockSpec((1, n_ctx, d), lambda bi, c=c: (bi, 0, c)) for c in range(3)],
        out_specs=pl.BlockSpec((1, n_ctx, d), lambda bi: (bi, 0, 0)),
        out_shape=jax.ShapeDtypeStruct((b, n_ctx, d), BF16),
        compiler_params=_params(1),
        name="ctx_attention",
    )(p_ctx, p_ctx, p_ctx)


def _post_kernel(a_ref, *rest, gated, mlp_chunk):
    if gated:
        g_ref, x_ref, mod_ref, wo_ref, w1_ref, w2_ref, ln_ref, o_ref = rest
    else:
        x_ref, mod_ref, wo_ref, w1_ref, w2_ref, ln_ref, o_ref = rest
    a = a_ref[0]
    if gated:
        a = (a.astype(F32) * _silu(g_ref[0].astype(F32))).astype(BF16)
    y = _dot(a, wo_ref[...])
    gate1 = mod_ref[0, 2:3, :]
    shift2 = mod_ref[0, 3:4, :]
    scale2 = mod_ref[0, 4:5, :]
    gate2 = mod_ref[0, 5:6, :]
    x1 = _layer_norm(DN_ALPHA * x_ref[0] + gate1 * y, ln_ref[0:1, :], ln_ref[1:2, :])
    h = (x1 * (1.0 + scale2) + shift2).astype(BF16)
    hidden = w1_ref.shape[1]
    acc = None
    for c in range(hidden // mlp_chunk):
        cols = slice(c * mlp_chunk, (c + 1) * mlp_chunk)
        u = jnp.maximum(_dot(h, w1_ref[:, cols]), 0.0)
        part = _dot((u * u).astype(BF16), w2_ref[cols, :])
        acc = part if acc is None else acc + part
    o_ref[0] = _layer_norm(DN_ALPHA * x1 + gate2 * acc, ln_ref[2:3, :], ln_ref[3:4, :])


def _post(a, gate_src, gate_blk, x, mods, mod_row, wo, w1, w2, ln, tm=512, mlp_chunk=1024):
    b, n, d = x.shape
    ka = a.shape[2]
    tm = min(tm, n)
    assert n % tm == 0
    gated = gate_src is not None
    if mod_row is None:
        mod_map = lambda bi, t: (bi, 0, 0)
    else:
        mod_map = lambda bi, t: (mod_row, 0, 0)
    in_specs = [pl.BlockSpec((1, tm, ka), lambda bi, t: (bi, t, 0))]
    args = [a]
    if gated:
        in_specs.append(pl.BlockSpec((1, tm, ka), lambda bi, t: (bi, t, gate_blk)))
        args.append(gate_src)
    in_specs += [
        pl.BlockSpec((1, tm, d), lambda bi, t: (bi, t, 0)),
        pl.BlockSpec((1, 6, d), mod_map),
        pl.BlockSpec(wo.shape, lambda bi, t: (0, 0)),
        pl.BlockSpec(w1.shape, lambda bi, t: (0, 0)),
        pl.BlockSpec(w2.shape, lambda bi, t: (0, 0)),
        pl.BlockSpec(ln.shape, lambda bi, t: (0, 0)),
    ]
    args += [x, mods, wo, w1, w2, ln]
    return pl.pallas_call(
        functools.partial(_post_kernel, gated=gated, mlp_chunk=mlp_chunk),
        grid=(b, n // tm),
        in_specs=in_specs,
        out_specs=pl.BlockSpec((1, tm, d), lambda bi, t: (bi, t, 0)),
        out_shape=jax.ShapeDtypeStruct((b, n, d), F32),
        compiler_params=_params(2),
        name="post",
    )(*args)


def _rope_tables(n, head_dim):
    axis_dim = head_dim // 2
    t = jnp.arange(n)
    row = (t // GRID_W).astype(F32)
    col = (t % GRID_W).astype(F32)
    inv = 1.0 / (ROPE_BASE ** (jnp.arange(0, axis_dim, 2, dtype=F32) / axis_dim))
    ar = row[:, None] * inv
    ac = col[:, None] * inv
    cos = jnp.concatenate([jnp.cos(ar), jnp.cos(ar), jnp.cos(ac), jnp.cos(ac)], axis=-1)
    sin = jnp.concatenate([-jnp.sin(ar), jnp.sin(ar), -jnp.sin(ac), jnp.sin(ac)], axis=-1)
    return cos, sin


def kernel(x, c, ctx, c_ctx, ada_w, ada_b, ret_w_in, ret_w_o, ret_decay,
           nat_w_in, nat_w_o, nat_rpb, mlp_w1, mlp_w2, ln_g, ln_b):
    b, n, d = x.shape
    mods_all = _ada_mods(c, c_ctx, ada_w, ada_b)
    ctx_row = b

    qk_w = d
    dk = qk_w // RET_HEADS
    rope = _rope_tables(n, dk)
    qk_cs = jnp.concatenate([jnp.ones((qk_w,), F32), jnp.full((qk_w,), dk ** -0.5, F32)])[None, :]
    vg_cs = jnp.ones((1, ret_w_in.shape[2] - 2 * qk_w), F32)
    nat_cs = jnp.concatenate([jnp.full((d,), NAT_HEAD_DIM ** -0.5 * LOG2E, F32),
                              jnp.ones((2 * d,), F32)])[None, :]

    for i in range(DEPTH):
        last = i == DEPTH - 1
        j = i // N_MIXERS
        mods = mods_all[i]
        ln = jnp.stack([ln_g[i, 0], ln_b[i, 0], ln_g[i, 1], ln_b[i, 1]])
        w1 = mlp_w1[i].astype(BF16)
        w2 = mlp_w2[i].astype(BF16)
        if i % N_MIXERS == 0:
            w_qk = ret_w_in[j, :, :2 * qk_w].astype(BF16)
            w_vg = ret_w_in[j, :, 2 * qk_w:].astype(BF16)
            wo = ret_w_o[j].astype(BF16)
            qk_lat = _project(x, mods, None, w_qk, qk_cs, rope=rope)
            vg_lat = _project(x, mods, None, w_vg, vg_cs, tn=2048)
            qk_ctx = _project(ctx, mods, ctx_row, w_qk, qk_cs)
            vg_ctx = _project(ctx, mods, ctx_row, w_vg, vg_cs, tn=2048)
            a_lat, a_ctx = _retention(qk_lat, vg_lat, qk_ctx, vg_ctx, ret_decay[j])
            gate_blk = 1
            g_lat, g_ctx = vg_lat, vg_ctx
        else:
            w_in = nat_w_in[j].astype(BF16)
            wo = nat_w_o[j].astype(BF16)
            p_lat = _project(x, mods, None, w_in, nat_cs, tn=1536)
            p_ctx = _project(ctx, mods, ctx_row, w_in, nat_cs, tn=1536)
            a_lat = _nat_attention(p_lat, p_ctx, _nat_bias_table(nat_rpb[j]))
            a_ctx = None if last else _ctx_attention(p_ctx)
            gate_blk = 0
            g_lat = g_ctx = None
        x = _post(a_lat, g_lat, gate_blk, x, mods, None, wo, w1, w2, ln)
        if not last:
            ctx = _post(a_ctx, g_ctx, gate_blk, ctx, mods, ctx_row, wo, w1, w2, ln)
    return x
```

```python
import functools

import jax
import jax.numpy as jnp
import numpy as np
from jax import lax
from jax.experimental import pallas as pl
from jax.experimental.pallas import tpu as pltpu

F32 = jnp.float32
BF16 = jnp.bfloat16

DEPTH = 4
GRID_W = 64
N_MIXERS = 2
RET_HEADS = 4
RET_CHUNK = 256
NAT_HEAD_DIM = 32
NAT_WIN_ROWS = 8
NAT_WIN_COLS = 16
ROPE_BASE = 10000.0
LN_EPS = 1e-5
DN_ALPHA = (2.0 * DEPTH) ** 0.25
LOG2E = 1.4426950408889634

LANES = 128
MXU_DIM = 256
VMEM_LIMIT = 56 * 1024 * 1024
MOD_ROWS = 16


def _params(n_axes, vmem=VMEM_LIMIT):
    return pltpu.CompilerParams(
        dimension_semantics=("arbitrary",) * n_axes, vmem_limit_bytes=vmem)


def _silu(x):
    return x / (1.0 + jnp.exp(-x))


def _layer_norm(z, g, b):
    mu = jnp.mean(z, axis=-1, keepdims=True)
    zc = z - mu
    var = jnp.mean(zc * zc, axis=-1, keepdims=True)
    return zc * lax.rsqrt(var + LN_EPS) * g + b


def _dot(a, b):
    return jnp.dot(a, b, preferred_element_type=F32)


def _dot_nt(a, b):
    return lax.dot_general(a, b, (((1,), (1,)), ((), ())), preferred_element_type=F32)


def _ada_kernel(c_ref, w_ref, b_ref, o_ref):
    s = _silu(c_ref[...])
    o_ref[0] = jnp.dot(s, w_ref[0], preferred_element_type=F32,
                       precision=lax.Precision.HIGHEST) + b_ref[0]


def _ada_mods(c, c_ctx, ada_w, ada_b):
    depth, d, d6 = ada_w.shape
    b = c.shape[0]
    cc = jnp.zeros((MOD_ROWS, d), F32).at[:b].set(c).at[b].set(c_ctx)
    tn = d6 // 4
    out = pl.pallas_call(
        _ada_kernel,
        grid=(depth, d6 // tn),
        in_specs=[
            pl.BlockSpec((MOD_ROWS, d), lambda i, j: (0, 0)),
            pl.BlockSpec((1, d, tn), lambda i, j: (i, 0, j)),
            pl.BlockSpec((1, 1, tn), lambda i, j: (i, 0, j)),
        ],
        out_specs=pl.BlockSpec((1, MOD_ROWS, tn), lambda i, j: (i, 0, j)),
        out_shape=jax.ShapeDtypeStruct((depth, MOD_ROWS, d6), F32),
        compiler_params=_params(2),
        name="ada_mods",
    )(cc, ada_w, ada_b.reshape(depth, 1, d6))
    return out.reshape(depth, MOD_ROWS, 6, d)


def _proj_kernel(x_ref, mod_ref, w_ref, cs_ref, *rest, rope, tn):
    if rope:
        cos_ref, sin_ref, o_ref = rest
    else:
        (o_ref,) = rest
    shift = mod_ref[0, 0:1, :]
    scale = mod_ref[0, 1:2, :]
    h = (x_ref[0] * (1.0 + scale) + shift).astype(BF16)
    for j in range(w_ref.shape[1] // tn):
        cols = slice(j * tn, (j + 1) * tn)
        acc = _dot(h, w_ref[:, cols]) * cs_ref[:, cols]
        if rope:
            for s in range(tn // LANES):
                seg = acc[:, s * LANES:(s + 1) * LANES]
                t = (s % 2) * LANES
                cos = cos_ref[:, t:t + LANES]
                sin = sin_ref[:, t:t + LANES]
                rot = seg * cos + pltpu.roll(seg, LANES // 2, 1) * sin
                o_ref[0, :, j * tn + s * LANES:j * tn + (s + 1) * LANES] = rot.astype(BF16)
        else:
            o_ref[0, :, cols] = acc.astype(BF16)


def _project(x, mods, mod_row, w, col_scale, rope=None, tm=1024, tn=1024):
    b, n, d = x.shape
    nout = w.shape[1]
    tm = min(tm, n)
    assert n % tm == 0 and nout % tn == 0
    if mod_row is None:
        mod_map = lambda bi, t: (bi, 0, 0)
    else:
        mod_map = lambda bi, t: (mod_row, 0, 0)
    in_specs = [
        pl.BlockSpec((1, tm, d), lambda bi, t: (bi, t, 0)),
        pl.BlockSpec((1, 6, d), mod_map),
        pl.BlockSpec((d, nout), lambda bi, t: (0, 0)),
        pl.BlockSpec((1, nout), lambda bi, t: (0, 0)),
    ]
    args = [x, mods, w, col_scale]
    if rope is not None:
        cos_t, sin_t = rope
        hw = cos_t.shape[1]
        assert tn % hw == 0
        in_specs += [pl.BlockSpec((tm, hw), lambda bi, t: (t, 0))] * 2
        args += [cos_t, sin_t]
    return pl.pallas_call(
        functools.partial(_proj_kernel, rope=rope is not None, tn=tn),
        grid=(b, n // tm),
        in_specs=in_specs,
        out_specs=pl.BlockSpec((1, tm, nout), lambda bi, t: (bi, t, 0)),
        out_shape=jax.ShapeDtypeStruct((b, n, nout), BF16),
        compiler_params=_params(2),
        name="proj",
    )(*args)


def _ret_kernel(dec_ref, qc_ref, kc_ref, vc_ref, ql_ref, kl_ref, vl_ref,
                ol_ref, oc_ref, raw_scr, st_scr, intra_scr, qd_scr, kd_scr, *, chunk):
    cs = chunk
    row = lax.broadcasted_iota(jnp.int32, (cs, cs), 0).astype(F32)
    col = lax.broadcasted_iota(jnp.int32, (cs, cs), 1).astype(F32)
    pos = lax.broadcasted_iota(jnp.int32, (cs, 1), 0).astype(F32)
    cdec = []
    for d in range(2):
        lg = -jnp.exp(dec_ref[0, d, 0:1, 0:1])
        rel = (row - col) if d == 0 else (col - row)
        intra_scr[d] = jnp.where(rel >= 0, jnp.exp(lg * jnp.maximum(rel, 0.0)), 0.0)
        if d == 0:
            qd_scr[d] = jnp.exp(lg * (pos + 1.0))
            kd_scr[d] = jnp.exp(lg * (cs - 1.0 - pos))
        else:
            qd_scr[d] = jnp.exp(lg * (cs - pos))
            kd_scr[d] = jnp.exp(lg * pos)
        cdec.append(jnp.exp(lg * cs))
    st_scr[...] = jnp.zeros_like(st_scr)

    def normed(o):
        mu = jnp.mean(o, axis=-1, keepdims=True)
        oc = o - mu
        var = jnp.mean(oc * oc, axis=-1, keepdims=True)
        return (oc * lax.rsqrt(var + LN_EPS)).astype(BF16)

    def segment(q_ref, k_ref, v_ref, o_ref):
        nch = q_ref.shape[1] // cs

        def chunk_out(d, off):
            q = q_ref[0, pl.ds(off, cs), :]
            k = k_ref[0, pl.ds(off, cs), :]
            v = v_ref[0, pl.ds(off, cs), :]
            st = st_scr[d]
            s = _dot_nt(q, k) * intra_scr[d]
            o = _dot(s.astype(BF16), v) + qd_scr[d] * _dot(q, st.astype(BF16))
            kdt = (k.astype(F32) * kd_scr[d]).T.astype(BF16)
            st_scr[d] = st * cdec[d] + _dot(kdt, v)
            return o

        if nch == 1:
            o_ref[0] = normed(chunk_out(0, 0) + chunk_out(1, 0))
            return
        assert nch % 2 == 0

        def offsets(i):
            return pl.multiple_of(i * cs, cs), pl.multiple_of((nch - 1 - i) * cs, cs)

        def first_half(i, carry):
            off_f, off_b = offsets(i)
            raw_scr[pl.ds(off_f, cs), :] = chunk_out(0, off_f)
            raw_scr[pl.ds(off_b, cs), :] = chunk_out(1, off_b)
            return carry

        def second_half(i, carry):
            off_f, off_b = offsets(i)
            o_ref[0, pl.ds(off_f, cs), :] = normed(chunk_out(0, off_f) + raw_scr[pl.ds(off_f, cs), :])
            o_ref[0, pl.ds(off_b, cs), :] = normed(chunk_out(1, off_b) + raw_scr[pl.ds(off_b, cs), :])
            return carry

        unroll = 2 if (nch // 2) % 2 == 0 else 1
        lax.fori_loop(0, nch // 2, first_half, 0, unroll=unroll)
        lax.fori_loop(nch // 2, nch, second_half, 0, unroll=unroll)

    segment(qc_ref, kc_ref, vc_ref, oc_ref)
    segment(ql_ref, kl_ref, vl_ref, ol_ref)


def _retention(qk_lat, vg_lat, qk_ctx, vg_ctx, decay, chunk=RET_CHUNK):
    b, n_lat, width = qk_lat.shape
    n_ctx = qk_ctx.shape[1]
    nh = RET_HEADS
    dk = width // (2 * nh)
    dv = vg_lat.shape[2] // (2 * nh)
    assert n_ctx % chunk == 0 and n_lat % chunk == 0
    dec = jnp.broadcast_to(decay.astype(F32).T[:, :, None, None], (nh, 2, 8, LANES))

    def specs(n):
        return [
            pl.BlockSpec((1, n, dk), lambda bi, h: (bi, 0, h)),
            pl.BlockSpec((1, n, dk), lambda bi, h: (bi, 0, nh + h)),
            pl.BlockSpec((1, n, dv), lambda bi, h: (bi, 0, h)),
        ]

    return pl.pallas_call(
        functools.partial(_ret_kernel, chunk=chunk),
        grid=(b, nh),
        in_specs=[pl.BlockSpec((1, 2, 8, LANES), lambda bi, h: (h, 0, 0, 0))]
        + specs(n_ctx) + specs(n_lat),
        out_specs=[
            pl.BlockSpec((1, n_lat, dv), lambda bi, h: (bi, 0, h)),
            pl.BlockSpec((1, n_ctx, dv), lambda bi, h: (bi, 0, h)),
        ],
        out_shape=[
            jax.ShapeDtypeStruct((b, n_lat, nh * dv), BF16),
            jax.ShapeDtypeStruct((b, n_ctx, nh * dv), BF16),
        ],
        scratch_shapes=[
            pltpu.VMEM((max(n_ctx, n_lat), dv), F32),
            pltpu.VMEM((2, dk, dv), F32),
            pltpu.VMEM((2, chunk, chunk), F32),
            pltpu.VMEM((2, chunk, 1), F32),
            pltpu.VMEM((2, chunk, 1), F32),
        ],
        compiler_params=_params(2),
        name="retention",
    )(dec, qk_ctx, qk_ctx, vg_ctx, qk_lat, qk_lat, vg_lat)


HEADS_PER_GROUP = MXU_DIM // NAT_HEAD_DIM


def _group_attention(qg, keys, vals, biases):
    nq = qg.shape[0]
    lane_head = lax.broadcasted_iota(jnp.int32, (nq, MXU_DIM), 1) // NAT_HEAD_DIM
    zero = jnp.zeros_like(qg)
    qm = jnp.concatenate(
        [jnp.where(lane_head == h, qg, zero) for h in range(HEADS_PER_GROUP)], axis=0)
    scores = []
    for k, bias in zip(keys, biases):
        s = _dot_nt(qm, k)
        scores.append(s if bias is None else s + bias)

    def lane_fold(parts, op):
        acc = None
        for a in parts:
            for c in range(0, a.shape[1], MXU_DIM):
                blk = a[:, c:c + MXU_DIM]
                acc = blk if acc is None else op(acc, blk)
        return acc

    m = lane_fold(scores, jnp.maximum).max(axis=-1, keepdims=True)
    probs = [jnp.exp2(s - m) for s in scores]
    l = lane_fold(probs, jnp.add).sum(axis=-1, keepdims=True)
    r = None
    for p, v in zip(probs, vals):
        pv = _dot(p.astype(BF16), v)
        r = pv if r is None else r + pv
    r = r * (1.0 / l)
    out = r[0:nq]
    for h in range(1, HEADS_PER_GROUP):
        out = jnp.where(lane_head == h, r[h * nq:(h + 1) * nq], out)
    return out


def _nat_kernel(q_ref, k_ref, v_ref, kc_ref, vc_ref, bias_ref, o_ref, *, rows_per_step, rows):
    kh = NAT_WIN_ROWS
    w = GRID_W
    n_groups = q_ref.shape[2] // MXU_DIM
    t = pl.program_id(1)

    def row_body(rr, carry):
        r = t * rows_per_step + rr
        start = jnp.clip(r - kh // 2, 0, rows - kh)
        koff = pl.multiple_of(start * w, w)
        qoff = pl.multiple_of(rr * w, w)
        dr0 = start - r + NAT_WIN_ROWS - 1
        for g in range(n_groups):
            lanes = slice(g * MXU_DIM, (g + 1) * MXU_DIM)
            qg = q_ref[0, pl.ds(qoff, w), lanes]
            kb = k_ref[0, pl.ds(koff, kh * w), lanes]
            vb = v_ref[0, pl.ds(koff, kh * w), lanes]
            bias = jnp.concatenate([
                jnp.concatenate([bias_ref[g * HEADS_PER_GROUP + h, dr0 + i]
                                 for i in range(0, kh, 2)], axis=1)
                for h in range(HEADS_PER_GROUP)], axis=0)
            out = _group_attention(qg, [kb, kc_ref[0, :, lanes]], [vb, vc_ref[0, :, lanes]],
                                   [bias, None])
            o_ref[0, pl.ds(qoff, w), lanes] = out.astype(BF16)
        return carry

    lax.fori_loop(0, rows_per_step, row_body, 0, unroll=4)


def _nat_bias_table(rpb):
    kw = NAT_WIN_COLS
    cols = jnp.arange(GRID_W)
    col_start = jnp.clip(cols - kw // 2, 0, GRID_W - kw)
    col_in = (cols[None, :] >= col_start[:, None]) & (cols[None, :] < col_start[:, None] + kw)
    dc_idx = jnp.clip(cols[None, :] - cols[:, None] + kw - 1, 0, 2 * kw - 2)
    full = rpb.astype(F32)[:, :, dc_idx]
    full = jnp.where(col_in[None, None], full * LOG2E, -jnp.inf)
    return jnp.concatenate([full[:, :-1], full[:, 1:]], axis=-1)


def _nat_attention(p_lat, p_ctx, bias2, rows_per_step=8):
    b, n, d3 = p_lat.shape
    d = d3 // 3
    n_ctx = p_ctx.shape[1]
    rows = n // GRID_W
    assert rows % rows_per_step == 0 and rows >= NAT_WIN_ROWS
    tq = rows_per_step * GRID_W
    return pl.pallas_call(
        functools.partial(_nat_kernel, rows_per_step=rows_per_step, rows=rows),
        grid=(b, rows // rows_per_step),
        in_specs=[
            pl.BlockSpec((1, tq, d), lambda bi, t: (bi, t, 0)),
            pl.BlockSpec((1, n, d), lambda bi, t: (bi, 0, 1), pipeline_mode=pl.Buffered(1)),
            pl.BlockSpec((1, n, d), lambda bi, t: (bi, 0, 2), pipeline_mode=pl.Buffered(1)),
            pl.BlockSpec((1, n_ctx, d), lambda bi, t: (bi, 0, 1)),
            pl.BlockSpec((1, n_ctx, d), lambda bi, t: (bi, 0, 2)),
            pl.BlockSpec(bias2.shape, lambda bi, t: (0, 0, 0, 0)),
        ],
        out_specs=pl.BlockSpec((1, tq, d), lambda bi, t: (bi, t, 0)),
        out_shape=jax.ShapeDtypeStruct((b, n, d), BF16),
        compiler_params=_params(2),
        name="nat_attention",
    )(p_lat, p_lat, p_lat, p_ctx, p_ctx, bias2)


def _ctx_attn_kernel(q_ref, k_ref, v_ref, o_ref, *, tq):
    n_groups = q_ref.shape[2] // MXU_DIM
    for qb in range(q_ref.shape[1] // tq):
        for g in range(n_groups):
            lanes = slice(g * MXU_DIM, (g + 1) * MXU_DIM)
            out = _group_attention(q_ref[0, qb * tq:(qb + 1) * tq, lanes],
                                   [k_ref[0, :, lanes]], [v_ref[0, :, lanes]], [None])
            o_ref[0, qb * tq:(qb + 1) * tq, lanes] = out.astype(BF16)


def _ctx_attention(p_ctx, tq=64):
    b, n_ctx, d3 = p_ctx.shape
    d = d3 // 3
    return pl.pallas_call(
        functools.partial(_ctx_attn_kernel, tq=tq),
        grid=(b,),
        in_specs=[pl.BlockSpec((1, n_ctx, d), lambda bi, c=c: (bi, 0, c)) for c in range(3)],
        out_specs=pl.BlockSpec((1, n_ctx, d), lambda bi: (bi, 0, 0)),
        out_shape=jax.ShapeDtypeStruct((b, n_ctx, d), BF16),
        compiler_params=_params(1),
        name="ctx_attention",
    )(p_ctx, p_ctx, p_ctx)


def _post_kernel(a_ref, *rest, gated, mlp_chunk, n_sub):
    if gated:
        g_ref, x_ref, mod_ref, wo_ref, w1_ref, w2_ref, ln_ref, o_ref = rest
    else:
        x_ref, mod_ref, wo_ref, w1_ref, w2_ref, ln_ref, o_ref = rest
    gate1 = mod_ref[0, 2:3, :]
    shift2 = mod_ref[0, 3:4, :]
    scale2 = mod_ref[0, 4:5, :]
    gate2 = mod_ref[0, 5:6, :]
    hidden = w1_ref.shape[1]
    sub = x_ref.shape[1] // n_sub
    rows = [slice(s * sub, (s + 1) * sub) for s in range(n_sub)]

    def out_proj(s):
        a = a_ref[0, rows[s], :]
        if gated:
            a = (a.astype(F32) * _silu(g_ref[0, rows[s], :].astype(F32))).astype(BF16)
        return _dot(a, wo_ref[...])

    def norm1(s, y):
        x1 = _layer_norm(DN_ALPHA * x_ref[0, rows[s], :] + gate1 * y,
                         ln_ref[0:1, :], ln_ref[1:2, :])
        return x1, (x1 * (1.0 + scale2) + shift2).astype(BF16)

    def mlp(h):
        acc = None
        for c in range(hidden // mlp_chunk):
            cols = slice(c * mlp_chunk, (c + 1) * mlp_chunk)
            u = jnp.maximum(_dot(h, w1_ref[:, cols]), 0.0)
            part = _dot((u * u).astype(BF16), w2_ref[cols, :])
            acc = part if acc is None else acc + part
        return acc

    ys = [out_proj(s) for s in range(n_sub)]
    x1, h = norm1(0, ys[0])
    for s in range(n_sub):
        acc = mlp(h)
        x1_done = x1
        if s + 1 < n_sub:
            x1, h = norm1(s + 1, ys[s + 1])
        o_ref[0, rows[s], :] = _layer_norm(DN_ALPHA * x1_done + gate2 * acc,
                                           ln_ref[2:3, :], ln_ref[3:4, :])


def _post(a, gate_src, gate_blk, x, mods, mod_row, wo, w1, w2, ln, tm=512, mlp_chunk=1024,
          n_sub=2):
    b, n, d = x.shape
    ka = a.shape[2]
    tm = min(tm, n)
    assert n % tm == 0
    gated = gate_src is not None
    if mod_row is None:
        mod_map = lambda bi, t: (bi, 0, 0)
    else:
        mod_map = lambda bi, t: (mod_row, 0, 0)
    in_specs = [pl.BlockSpec((1, tm, ka), lambda bi, t: (bi, t, 0))]
    args = [a]
    if gated:
        in_specs.append(pl.BlockSpec((1, tm, ka), lambda bi, t: (bi, t, gate_blk)))
        args.append(gate_src)
    in_specs += [
        pl.BlockSpec((1, tm, d), lambda bi, t: (bi, t, 0)),
        pl.BlockSpec((1, 6, d), mod_map),
        pl.BlockSpec(wo.shape, lambda bi, t: (0, 0)),
        pl.BlockSpec(w1.shape, lambda bi, t: (0, 0)),
        pl.BlockSpec(w2.shape, lambda bi, t: (0, 0)),
        pl.BlockSpec(ln.shape, lambda bi, t: (0, 0)),
    ]
    args += [x, mods, wo, w1, w2, ln]
    return pl.pallas_call(
        functools.partial(_post_kernel, gated=gated, mlp_chunk=mlp_chunk, n_sub=n_sub),
        grid=(b, n // tm),
        in_specs=in_specs,
        out_specs=pl.BlockSpec((1, tm, d), lambda bi, t: (bi, t, 0)),
        out_shape=jax.ShapeDtypeStruct((b, n, d), F32),
        compiler_params=_params(2),
        name="post",
    )(*args)


def _rope_tables(n, head_dim):
    axis_dim = head_dim // 2
    t = jnp.arange(n)
    row = (t // GRID_W).astype(F32)
    col = (t % GRID_W).astype(F32)
    inv = 1.0 / (ROPE_BASE ** (jnp.arange(0, axis_dim, 2, dtype=F32) / axis_dim))
    ar = row[:, None] * inv
    ac = col[:, None] * inv
    cos = jnp.concatenate([jnp.cos(ar), jnp.cos(ar), jnp.cos(ac), jnp.cos(ac)], axis=-1)
    sin = jnp.concatenate([-jnp.sin(ar), jnp.sin(ar), -jnp.sin(ac), jnp.sin(ac)], axis=-1)
    return cos, sin


def kernel(x, c, ctx, c_ctx, ada_w, ada_b, ret_w_in, ret_w_o, ret_decay,
           nat_w_in, nat_w_o, nat_rpb, mlp_w1, mlp_w2, ln_g, ln_b):
    b, n, d = x.shape
    n_ctx = ctx.shape[1]
    mods_all = _ada_mods(c, c_ctx, ada_w, ada_b)
    ctx_row = b
    ctx = ctx.reshape(1, b * n_ctx, d)

    qk_w = d
    dk = qk_w // RET_HEADS
    rope = _rope_tables(n, dk)
    qk_cs = jnp.concatenate([jnp.ones((qk_w,), F32), jnp.full((qk_w,), dk ** -0.5, F32)])[None, :]
    vg_cs = jnp.ones((1, ret_w_in.shape[2] - 2 * qk_w), F32)
    nat_cs = jnp.concatenate([jnp.full((d,), NAT_HEAD_DIM ** -0.5 * LOG2E, F32),
                              jnp.ones((2 * d,), F32)])[None, :]

    for i in range(DEPTH):
        last = i == DEPTH - 1
        j = i // N_MIXERS
        mods = mods_all[i]
        ln = jnp.stack([ln_g[i, 0], ln_b[i, 0], ln_g[i, 1], ln_b[i, 1]])
        w1 = mlp_w1[i].astype(BF16)
        w2 = mlp_w2[i].astype(BF16)
        if i % N_MIXERS == 0:
            w_qk = ret_w_in[j, :, :2 * qk_w].astype(BF16)
            w_vg = ret_w_in[j, :, 2 * qk_w:].astype(BF16)
            wo = ret_w_o[j].astype(BF16)
            qk_lat = _project(x, mods, None, w_qk, qk_cs, rope=rope)
            vg_lat = _project(x, mods, None, w_vg, vg_cs)
            qk_ctx = _project(ctx, mods, ctx_row, w_qk, qk_cs)
            vg_ctx = _project(ctx, mods, ctx_row, w_vg, vg_cs)
            a_lat, a_ctx = _retention(qk_lat, vg_lat, qk_ctx.reshape(b, n_ctx, -1),
                                      vg_ctx.reshape(b, n_ctx, -1), ret_decay[j])
            gate_blk = 1
            g_lat, g_ctx = vg_lat, vg_ctx
        else:
            w_in = nat_w_in[j].astype(BF16)
            wo = nat_w_o[j].astype(BF16)
            p_lat = _project(x, mods, None, w_in, nat_cs)
            p_ctx = _project(ctx, mods, ctx_row, w_in, nat_cs).reshape(b, n_ctx, -1)
            a_lat = _nat_attention(p_lat, p_ctx, _nat_bias_table(nat_rpb[j]))
            a_ctx = None if last else _ctx_attention(p_ctx)
            gate_blk = 0
            g_lat = g_ctx = None
        x = _post(a_lat, g_lat, gate_blk, x, mods, None, wo, w1, w2, ln)
        if not last:
            ctx = _post(a_ctx.reshape(1, b * n_ctx, -1), g_ctx, gate_blk, ctx, mods, ctx_row,
                        wo, w1, w2, ln)
    return x
```

```python
import functools

import jax
import jax.numpy as jnp
import numpy as np
from jax import lax
from jax.experimental import pallas as pl
from jax.experimental.pallas import tpu as pltpu

F32 = jnp.float32
BF16 = jnp.bfloat16

DEPTH = 4
GRID_W = 64
N_MIXERS = 2
RET_HEADS = 4
RET_CHUNK = 256
NAT_HEAD_DIM = 32
NAT_WIN_ROWS = 8
NAT_WIN_COLS = 16
ROPE_BASE = 10000.0
LN_EPS = 1e-5
DN_ALPHA = (2.0 * DEPTH) ** 0.25
LOG2E = 1.4426950408889634

LANES = 128
MXU_DIM = 256
VMEM_LIMIT = 56 * 1024 * 1024
MOD_ROWS = 16


def _params(n_axes, vmem=VMEM_LIMIT):
    return pltpu.CompilerParams(
        dimension_semantics=("arbitrary",) * n_axes, vmem_limit_bytes=vmem)


def _silu(x):
    return x / (1.0 + jnp.exp(-x))


def _layer_norm(z, g, b):
    mu = jnp.mean(z, axis=-1, keepdims=True)
    zc = z - mu
    var = jnp.mean(zc * zc, axis=-1, keepdims=True)
    return zc * lax.rsqrt(var + LN_EPS) * g + b


def _dot(a, b):
    return jnp.dot(a, b, preferred_element_type=F32)


def _dot_nt(a, b):
    return lax.dot_general(a, b, (((1,), (1,)), ((), ())), preferred_element_type=F32)


def _ada_kernel(c_ref, w_ref, b_ref, o_ref):
    s = _silu(c_ref[...])
    o_ref[0] = jnp.dot(s, w_ref[0], preferred_element_type=F32,
                       precision=lax.Precision.HIGHEST) + b_ref[0]


def _ada_mods(c, c_ctx, ada_w, ada_b):
    depth, d, d6 = ada_w.shape
    b = c.shape[0]
    cc = jnp.zeros((MOD_ROWS, d), F32).at[:b].set(c).at[b].set(c_ctx)
    tn = d6 // 4
    out = pl.pallas_call(
        _ada_kernel,
        grid=(depth, d6 // tn),
        in_specs=[
            pl.BlockSpec((MOD_ROWS, d), lambda i, j: (0, 0)),
            pl.BlockSpec((1, d, tn), lambda i, j: (i, 0, j)),
            pl.BlockSpec((1, 1, tn), lambda i, j: (i, 0, j)),
        ],
        out_specs=pl.BlockSpec((1, MOD_ROWS, tn), lambda i, j: (i, 0, j)),
        out_shape=jax.ShapeDtypeStruct((depth, MOD_ROWS, d6), F32),
        compiler_params=_params(2),
        name="ada_mods",
    )(cc, ada_w, ada_b.reshape(depth, 1, d6))
    return out.reshape(depth, MOD_ROWS, 6, d)


def _proj_kernel(x_ref, mod_ref, w_ref, cs_ref, *rest, rope, tn):
    if rope:
        cos_ref, sin_ref, o_ref = rest
    else:
        (o_ref,) = rest
    shift = mod_ref[0, 0:1, :]
    scale = mod_ref[0, 1:2, :]
    h = (x_ref[0] * (1.0 + scale) + shift).astype(BF16)
    for j in range(w_ref.shape[1] // tn):
        cols = slice(j * tn, (j + 1) * tn)
        acc = _dot(h, w_ref[:, cols]) * cs_ref[:, cols]
        if rope:
            for s in range(tn // LANES):
                seg = acc[:, s * LANES:(s + 1) * LANES]
                t = (s % 2) * LANES
                cos = cos_ref[:, t:t + LANES]
                sin = sin_ref[:, t:t + LANES]
                rot = seg * cos + pltpu.roll(seg, LANES // 2, 1) * sin
                o_ref[0, :, j * tn + s * LANES:j * tn + (s + 1) * LANES] = rot.astype(BF16)
        else:
            o_ref[0, :, cols] = acc.astype(BF16)


def _project(x, mods, mod_row, w, col_scale, rope=None, tm=1024, tn=1024):
    b, n, d = x.shape
    nout = w.shape[1]
    tm = min(tm, n)
    assert n % tm == 0 and nout % tn == 0
    if mod_row is None:
        mod_map = lambda bi, t: (bi, 0, 0)
    else:
        mod_map = lambda bi, t: (mod_row, 0, 0)
    in_specs = [
        pl.BlockSpec((1, tm, d), lambda bi, t: (bi, t, 0)),
        pl.BlockSpec((1, 6, d), mod_map),
        pl.BlockSpec((d, nout), lambda bi, t: (0, 0)),
        pl.BlockSpec((1, nout), lambda bi, t: (0, 0)),
    ]
    args = [x, mods, w, col_scale]
    if rope is not None:
        cos_t, sin_t = rope
        hw = cos_t.shape[1]
        assert tn % hw == 0
        in_specs += [pl.BlockSpec((tm, hw), lambda bi, t: (t, 0))] * 2
        args += [cos_t, sin_t]
    return pl.pallas_call(
        functools.partial(_proj_kernel, rope=rope is not None, tn=tn),
        grid=(b, n // tm),
        in_specs=in_specs,
        out_specs=pl.BlockSpec((1, tm, nout), lambda bi, t: (bi, t, 0)),
        out_shape=jax.ShapeDtypeStruct((b, n, nout), BF16),
        compiler_params=_params(2),
        name="proj",
    )(*args)


def _ret_kernel(dec_ref, qc_ref, kc_ref, vc_ref, ql_ref, kl_ref, vl_ref,
                ol_ref, oc_ref, raw_scr, st_scr, intra_scr, qd_scr, kd_scr, *, chunk):
    cs = chunk
    row = lax.broadcasted_iota(jnp.int32, (cs, cs), 0).astype(F32)
    col = lax.broadcasted_iota(jnp.int32, (cs, cs), 1).astype(F32)
    pos = lax.broadcasted_iota(jnp.int32, (cs, 1), 0).astype(F32)
    cdec = []
    for d in range(2):
        lg = -jnp.exp(dec_ref[0, d, 0:1, 0:1])
        rel = (row - col) if d == 0 else (col - row)
        intra_scr[d] = jnp.where(rel >= 0, jnp.exp(lg * jnp.maximum(rel, 0.0)), 0.0)
        if d == 0:
            qd_scr[d] = jnp.exp(lg * (pos + 1.0))
            kd_scr[d] = jnp.exp(lg * (cs - 1.0 - pos))
        else:
            qd_scr[d] = jnp.exp(lg * (cs - pos))
            kd_scr[d] = jnp.exp(lg * pos)
        cdec.append(jnp.exp(lg * cs))
    st_scr[...] = jnp.zeros_like(st_scr)

    def normed(o):
        mu = jnp.mean(o, axis=-1, keepdims=True)
        oc = o - mu
        var = jnp.mean(oc * oc, axis=-1, keepdims=True)
        return (oc * lax.rsqrt(var + LN_EPS)).astype(BF16)

    def segment(q_ref, k_ref, v_ref, o_ref):
        nch = q_ref.shape[1] // cs

        def chunk_out(d, off):
            q = q_ref[0, pl.ds(off, cs), :]
            k = k_ref[0, pl.ds(off, cs), :]
            v = v_ref[0, pl.ds(off, cs), :]
            st = st_scr[d]
            s = _dot_nt(q, k) * intra_scr[d]
            o = _dot(s.astype(BF16), v) + qd_scr[d] * _dot(q, st.astype(BF16))
            kdt = (k.astype(F32) * kd_scr[d]).T.astype(BF16)
            st_scr[d] = st * cdec[d] + _dot(kdt, v)
            return o

        if nch == 1:
            o_ref[0] = normed(chunk_out(0, 0) + chunk_out(1, 0))
            return
        assert nch % 2 == 0

        def offsets(i):
            return pl.multiple_of(i * cs, cs), pl.multiple_of((nch - 1 - i) * cs, cs)

        def first_half(i, carry):
            off_f, off_b = offsets(i)
            raw_scr[pl.ds(off_f, cs), :] = chunk_out(0, off_f)
            raw_scr[pl.ds(off_b, cs), :] = chunk_out(1, off_b)
            return carry

        def second_half(i, carry):
            off_f, off_b = offsets(i)
            o_ref[0, pl.ds(off_f, cs), :] = normed(chunk_out(0, off_f) + raw_scr[pl.ds(off_f, cs), :])
            o_ref[0, pl.ds(off_b, cs), :] = normed(chunk_out(1, off_b) + raw_scr[pl.ds(off_b, cs), :])
            return carry

        unroll = 2 if (nch // 2) % 2 == 0 else 1
        lax.fori_loop(0, nch // 2, first_half, 0, unroll=unroll)
        lax.fori_loop(nch // 2, nch, second_half, 0, unroll=unroll)

    segment(qc_ref, kc_ref, vc_ref, oc_ref)
    segment(ql_ref, kl_ref, vl_ref, ol_ref)


def _retention(qk_lat, vg_lat, qk_ctx, vg_ctx, decay, chunk=RET_CHUNK):
    b, n_lat, width = qk_lat.shape
    n_ctx = qk_ctx.shape[1]
    nh = RET_HEADS
    dk = width // (2 * nh)
    dv = vg_lat.shape[2] // (2 * nh)
    assert n_ctx % chunk == 0 and n_lat % chunk == 0
    dec = jnp.broadcast_to(decay.astype(F32).T[:, :, None, None], (nh, 2, 8, LANES))

    def specs(n):
        return [
            pl.BlockSpec((1, n, dk), lambda bi, h: (bi, 0, h)),
            pl.BlockSpec((1, n, dk), lambda bi, h: (bi, 0, nh + h)),
            pl.BlockSpec((1, n, dv), lambda bi, h: (bi, 0, h)),
        ]

    return pl.pallas_call(
        functools.partial(_ret_kernel, chunk=chunk),
        grid=(b, nh),
        in_specs=[pl.BlockSpec((1, 2, 8, LANES), lambda bi, h: (h, 0, 0, 0))]
        + specs(n_ctx) + specs(n_lat),
        out_specs=[
            pl.BlockSpec((1, n_lat, dv), lambda bi, h: (bi, 0, h)),
            pl.BlockSpec((1, n_ctx, dv), lambda bi, h: (bi, 0, h)),
        ],
        out_shape=[
            jax.ShapeDtypeStruct((b, n_lat, nh * dv), BF16),
            jax.ShapeDtypeStruct((b, n_ctx, nh * dv), BF16),
        ],
        scratch_shapes=[
            pltpu.VMEM((max(n_ctx, n_lat), dv), F32),
            pltpu.VMEM((2, dk, dv), F32),
            pltpu.VMEM((2, chunk, chunk), F32),
            pltpu.VMEM((2, chunk, 1), F32),
            pltpu.VMEM((2, chunk, 1), F32),
        ],
        compiler_params=_params(2),
        name="retention",
    )(dec, qk_ctx, qk_ctx, vg_ctx, qk_lat, qk_lat, vg_lat)


HEADS_PER_GROUP = MXU_DIM // NAT_HEAD_DIM


def _group_scores(qg, keys, biases):
    nq = qg.shape[0]
    lane_head = lax.broadcasted_iota(jnp.int32, (nq, MXU_DIM), 1) // NAT_HEAD_DIM
    qm = jnp.concatenate(
        [jnp.where(lane_head == h, qg, 0.0) for h in range(HEADS_PER_GROUP)], axis=0).astype(BF16)
    scores = []
    for k, bias in zip(keys, biases):
        s = _dot_nt(qm, k)
        scores.append(s if bias is None else s + bias)
    return scores


def _group_values(scores, vals):
    nq = scores[0].shape[0] // HEADS_PER_GROUP
    lane_head = lax.broadcasted_iota(jnp.int32, (nq, MXU_DIM), 1) // NAT_HEAD_DIM

    def lane_fold(parts, op):
        acc = None
        for a in parts:
            for c in range(0, a.shape[1], MXU_DIM):
                blk = a[:, c:c + MXU_DIM]
                acc = blk if acc is None else op(acc, blk)
        return acc

    m = lane_fold(scores, jnp.maximum).max(axis=-1, keepdims=True)
    probs = [jnp.exp2(s - m) for s in scores]
    l = lane_fold(probs, jnp.add).sum(axis=-1, keepdims=True)
    r = None
    for p, v in zip(probs, vals):
        pv = _dot(p.astype(BF16), v)
        r = pv if r is None else r + pv
    r = r * (1.0 / l)
    out = r[0:nq]
    for h in range(1, HEADS_PER_GROUP):
        out = jnp.where(lane_head == h, r[h * nq:(h + 1) * nq], out)
    return out


NAT_BLOCK_KCOLS = 2 * NAT_WIN_COLS
NAT_QBLOCKS = ((0, 24, 0), (24, 16, 16), (40, 24, 32))
NAT_ROW_UNROLL = 4


def _nat_kernel(q_ref, k_ref, v_ref, kc_ref, vc_ref, bias_ref, o_ref, *, rows_per_step, rows):
    kh = NAT_WIN_ROWS
    w = GRID_W
    n_groups = q_ref.shape[2] // MXU_DIM
    t = pl.program_id(1)

    def band(ref, koff, k0, lanes):
        return jnp.concatenate(
            [ref[0, pl.ds(pl.multiple_of(koff + (i * w + k0), NAT_WIN_COLS), NAT_BLOCK_KCOLS), lanes]
             for i in range(kh)], axis=0)

    def score_stage(rr, g):
        r = t * rows_per_step + rr
        start = jnp.clip(r - kh // 2, 0, rows - kh)
        koff = pl.multiple_of(start * w, w)
        qoff = pl.multiple_of(rr * w, w)
        delta = r - start
        lanes = slice(g * MXU_DIM, (g + 1) * MXU_DIM)
        qf = q_ref[0, pl.ds(qoff, w), lanes].astype(F32)
        kcg = kc_ref[0, :, lanes]
        scores = []
        for q0, nq, k0 in NAT_QBLOCKS:
            bias = jnp.concatenate(
                [bias_ref[delta, g * HEADS_PER_GROUP + h, q0:q0 + nq, :]
                 for h in range(HEADS_PER_GROUP)], axis=0)
            scores.append(_group_scores(qf[q0:q0 + nq], [band(k_ref, koff, k0, lanes), kcg],
                                        [bias, None]))
        return scores, koff, qoff, lanes

    def value_stage(scores, koff, qoff, lanes):
        vcg = vc_ref[0, :, lanes]
        outs = [_group_values(s, [band(v_ref, koff, k0, lanes), vcg])
                for s, (q0, nq, k0) in zip(scores, NAT_QBLOCKS)]
        o_ref[0, pl.ds(qoff, w), lanes] = jnp.concatenate(outs, axis=0).astype(BF16)

    def rows_body(it, carry):
        units = [(it * NAT_ROW_UNROLL + j, g) for j in range(NAT_ROW_UNROLL)
                 for g in range(n_groups)]
        pending = None
        for rr, g in units:
            cur = score_stage(rr, g)
            if pending is not None:
                value_stage(*pending)
            pending = cur
        value_stage(*pending)
        return carry

    lax.fori_loop(0, rows_per_step // NAT_ROW_UNROLL, rows_body, 0)


def _nat_bias_table(rpb):
    kw, kh = NAT_WIN_COLS, NAT_WIN_ROWS
    cols = jnp.arange(GRID_W)
    col_start = jnp.clip(cols - kw // 2, 0, GRID_W - kw)
    k0_of_q = np.zeros((GRID_W,), np.int32)
    for q0, nq, k0 in NAT_QBLOCKS:
        k0_of_q[q0:q0 + nq] = k0
    kc = jnp.asarray(k0_of_q)[:, None] + jnp.arange(NAT_BLOCK_KCOLS)[None, :]
    valid = (kc >= col_start[:, None]) & (kc < col_start[:, None] + kw)
    dc_idx = jnp.clip(kc - cols[:, None] + kw - 1, 0, 2 * kw - 2)
    dr_idx = jnp.arange(kh)[None, :] + (kh - 1) - jnp.arange(kh)[:, None]
    tab = rpb.astype(F32)[:, dr_idx][:, :, :, dc_idx]
    tab = jnp.where(valid[None, None, None], tab * LOG2E, -jnp.inf)
    return tab.transpose(1, 0, 3, 2, 4).reshape(kh, rpb.shape[0], GRID_W, kh * NAT_BLOCK_KCOLS)


def _nat_attention(p_lat, p_ctx, bias2, rows_per_step=8):
    b, n, d3 = p_lat.shape
    d = d3 // 3
    n_ctx = p_ctx.shape[1]
    rows = n // GRID_W
    assert rows % rows_per_step == 0 and rows >= NAT_WIN_ROWS
    tq = rows_per_step * GRID_W
    return pl.pallas_call(
        functools.partial(_nat_kernel, rows_per_step=rows_per_step, rows=rows),
        grid=(b, rows // rows_per_step),
        in_specs=[
            pl.BlockSpec((1, tq, d), lambda bi, t: (bi, t, 0)),
            pl.BlockSpec((1, n, d), lambda bi, t: (bi, 0, 1), pipeline_mode=pl.Buffered(1)),
            pl.BlockSpec((1, n, d), lambda bi, t: (bi, 0, 2), pipeline_mode=pl.Buffered(1)),
            pl.BlockSpec((1, n_ctx, d), lambda bi, t: (bi, 0, 1)),
            pl.BlockSpec((1, n_ctx, d), lambda bi, t: (bi, 0, 2)),
            pl.BlockSpec(bias2.shape, lambda bi, t: (0, 0, 0, 0)),
        ],
        out_specs=pl.BlockSpec((1, tq, d), lambda bi, t: (bi, t, 0)),
        out_shape=jax.ShapeDtypeStruct((b, n, d), BF16),
        compiler_params=_params(2),
        name="nat_attention",
    )(p_lat, p_lat, p_lat, p_ctx, p_ctx, bias2)


def _ctx_attn_kernel(q_ref, k_ref, v_ref, o_ref, *, tq):
    n_groups = q_ref.shape[2] // MXU_DIM
    for qb in range(q_ref.shape[1] // tq):
        for g in range(n_groups):
            lanes = slice(g * MXU_DIM, (g + 1) * MXU_DIM)
            scores = _group_scores(q_ref[0, qb * tq:(qb + 1) * tq, lanes].astype(F32),
                                   [k_ref[0, :, lanes]], [None])
            out = _group_values(scores, [v_ref[0, :, lanes]])
            o_ref[0, qb * tq:(qb + 1) * tq, lanes] = out.astype(BF16)


def _ctx_attention(p_ctx, tq=64):
    b, n_ctx, d3 = p_ctx.shape
    d = d3 // 3
    return pl.pallas_call(
        functools.partial(_ctx_attn_kernel, tq=tq),
        grid=(b,),
        in_specs=[pl.BlockSpec((1, n_ctx, d), lambda bi, c=c: (bi, 0, c)) for c in range(3)],
        out_specs=pl.BlockSpec((1, n_ctx, d), lambda bi: (bi, 0, 0)),
        out_shape=jax.ShapeDtypeStruct((b, n_ctx, d), BF16),
        compiler_params=_params(1),
        name="ctx_attention",
    )(p_ctx, p_ctx, p_ctx)


def _post_kernel(a_ref, *rest, gated, mlp_chunk, n_sub):
    if gated:
        g_ref, x_ref, mod_ref, wo_ref, w1_ref, w2_ref, ln_ref, o_ref = rest
    else:
        x_ref, mod_ref, wo_ref, w1_ref, w2_ref, ln_ref, o_ref = rest
    gate1 = mod_ref[0, 2:3, :]
    shift2 = mod_ref[0, 3:4, :]
    scale2 = mod_ref[0, 4:5, :]
    gate2 = mod_ref[0, 5:6, :]
    hidden = w1_ref.shape[1]
    sub = x_ref.shape[1] // n_sub
    rows = [slice(s * sub, (s + 1) * sub) for s in range(n_sub)]

    def out_proj(s):
        a = a_ref[0, rows[s], :]
        if gated:
            a = (a.astype(F32) * _silu(g_ref[0, rows[s], :].astype(F32))).astype(BF16)
        return _dot(a, wo_ref[...])

    def norm1(s, y):
        x1 = _layer_norm(DN_ALPHA * x_ref[0, rows[s], :] + gate1 * y,
                         ln_ref[0:1, :], ln_ref[1:2, :])
        return x1, (x1 * (1.0 + scale2) + shift2).astype(BF16)

    def mlp(h):
        acc = None
        for c in range(hidden // mlp_chunk):
            cols = slice(c * mlp_chunk, (c + 1) * mlp_chunk)
            u = jnp.maximum(_dot(h, w1_ref[:, cols]), 0.0)
            part = _dot((u * u).astype(BF16), w2_ref[cols, :])
            acc = part if acc is None else acc + part
        return acc

    ys = [out_proj(s) for s in range(n_sub)]
    x1, h = norm1(0, ys[0])
    for s in range(n_sub):
        acc = mlp(h)
        x1_done = x1
        if s + 1 < n_sub:
            x1, h = norm1(s + 1, ys[s + 1])
        o_ref[0, rows[s], :] = _layer_norm(DN_ALPHA * x1_done + gate2 * acc,
                                           ln_ref[2:3, :], ln_ref[3:4, :])


def _post(a, gate_src, gate_blk, x, mods, mod_row, wo, w1, w2, ln, tm=512, mlp_chunk=1024,
          n_sub=2):
    b, n, d = x.shape
    ka = a.shape[2]
    tm = min(tm, n)
    assert n % tm == 0
    gated = gate_src is not None
    if mod_row is None:
        mod_map = lambda bi, t: (bi, 0, 0)
    else:
        mod_map = lambda bi, t: (mod_row, 0, 0)
    in_specs = [pl.BlockSpec((1, tm, ka), lambda bi, t: (bi, t, 0))]
    args = [a]
    if gated:
        in_specs.append(pl.BlockSpec((1, tm, ka), lambda bi, t: (bi, t, gate_blk)))
        args.append(gate_src)
    in_specs += [
        pl.BlockSpec((1, tm, d), lambda bi, t: (bi, t, 0)),
        pl.BlockSpec((1, 6, d), mod_map),
        pl.BlockSpec(wo.shape, lambda bi, t: (0, 0)),
        pl.BlockSpec(w1.shape, lambda bi, t: (0, 0)),
        pl.BlockSpec(w2.shape, lambda bi, t: (0, 0)),
        pl.BlockSpec(ln.shape, lambda bi, t: (0, 0)),
    ]
    args += [x, mods, wo, w1, w2, ln]
    return pl.pallas_call(
        functools.partial(_post_kernel, gated=gated, mlp_chunk=mlp_chunk, n_sub=n_sub),
        grid=(b, n // tm),
        in_specs=in_specs,
        out_specs=pl.BlockSpec((1, tm, d), lambda bi, t: (bi, t, 0)),
        out_shape=jax.ShapeDtypeStruct((b, n, d), F32),
        compiler_params=_params(2),
        name="post",
    )(*args)


def _rope_tables(n, head_dim):
    axis_dim = head_dim // 2
    t = jnp.arange(n)
    row = (t // GRID_W).astype(F32)
    col = (t % GRID_W).astype(F32)
    inv = 1.0 / (ROPE_BASE ** (jnp.arange(0, axis_dim, 2, dtype=F32) / axis_dim))
    ar = row[:, None] * inv
    ac = col[:, None] * inv
    cos = jnp.concatenate([jnp.cos(ar), jnp.cos(ar), jnp.cos(ac), jnp.cos(ac)], axis=-1)
    sin = jnp.concatenate([-jnp.sin(ar), jnp.sin(ar), -jnp.sin(ac), jnp.sin(ac)], axis=-1)
    return cos, sin


def kernel(x, c, ctx, c_ctx, ada_w, ada_b, ret_w_in, ret_w_o, ret_decay,
           nat_w_in, nat_w_o, nat_rpb, mlp_w1, mlp_w2, ln_g, ln_b):
    b, n, d = x.shape
    n_ctx = ctx.shape[1]
    mods_all = _ada_mods(c, c_ctx, ada_w, ada_b)
    ctx_row = b
    ctx = ctx.reshape(1, b * n_ctx, d)

    qk_w = d
    dk = qk_w // RET_HEADS
    rope = _rope_tables(n, dk)
    qk_cs = jnp.concatenate([jnp.ones((qk_w,), F32), jnp.full((qk_w,), dk ** -0.5, F32)])[None, :]
    vg_cs = jnp.ones((1, ret_w_in.shape[2] - 2 * qk_w), F32)
    nat_cs = jnp.concatenate([jnp.full((d,), NAT_HEAD_DIM ** -0.5 * LOG2E, F32),
                              jnp.ones((2 * d,), F32)])[None, :]

    for i in range(DEPTH):
        last = i == DEPTH - 1
        j = i // N_MIXERS
        mods = mods_all[i]
        ln = jnp.stack([ln_g[i, 0], ln_b[i, 0], ln_g[i, 1], ln_b[i, 1]])
        w1 = mlp_w1[i].astype(BF16)
        w2 = mlp_w2[i].astype(BF16)
        if i % N_MIXERS == 0:
            w_qk = ret_w_in[j, :, :2 * qk_w].astype(BF16)
            w_vg = ret_w_in[j, :, 2 * qk_w:].astype(BF16)
            wo = ret_w_o[j].astype(BF16)
            qk_lat = _project(x, mods, None, w_qk, qk_cs, rope=rope)
            vg_lat = _project(x, mods, None, w_vg, vg_cs)
            qk_ctx = _project(ctx, mods, ctx_row, w_qk, qk_cs)
            vg_ctx = _project(ctx, mods, ctx_row, w_vg, vg_cs)
            a_lat, a_ctx = _retention(qk_lat, vg_lat, qk_ctx.reshape(b, n_ctx, -1),
                                      vg_ctx.reshape(b, n_ctx, -1), ret_decay[j])
            gate_blk = 1
            g_lat, g_ctx = vg_lat, vg_ctx
        else:
            w_in = nat_w_in[j].astype(BF16)
            wo = nat_w_o[j].astype(BF16)
            p_lat = _project(x, mods, None, w_in, nat_cs)
            p_ctx = _project(ctx, mods, ctx_row, w_in, nat_cs).reshape(b, n_ctx, -1)
            a_lat = _nat_attention(p_lat, p_ctx, _nat_bias_table(nat_rpb[j]))
            a_ctx = None if last else _ctx_attention(p_ctx)
            gate_blk = 0
            g_lat = g_ctx = None
        x = _post(a_lat, g_lat, gate_blk, x, mods, None, wo, w1, w2, ln)
        if not last:
            ctx = _post(a_ctx.reshape(1, b * n_ctx, -1), g_ctx, gate_blk, ctx, mods, ctx_row,
                        wo, w1, w2, ln)
    return x
```

```python
import functools

import jax
import jax.numpy as jnp
import numpy as np
from jax import lax
from jax.experimental import pallas as pl
from jax.experimental.pallas import tpu as pltpu

F32 = jnp.float32
BF16 = jnp.bfloat16

DEPTH = 4
GRID_W = 64
N_MIXERS = 2
RET_HEADS = 4
RET_CHUNK = 256
NAT_HEAD_DIM = 32
NAT_WIN_ROWS = 8
NAT_WIN_COLS = 16
ROPE_BASE = 10000.0
LN_EPS = 1e-5
DN_ALPHA = (2.0 * DEPTH) ** 0.25
LOG2E = 1.4426950408889634

LANES = 128
MXU_DIM = 256
VMEM_LIMIT = 56 * 1024 * 1024
MOD_ROWS = 16


def _params(n_axes, vmem=VMEM_LIMIT):
    return pltpu.CompilerParams(
        dimension_semantics=("arbitrary",) * n_axes, vmem_limit_bytes=vmem)


def _silu(x):
    return x / (1.0 + jnp.exp(-x))


def _layer_norm(z, g, b):
    mu = jnp.mean(z, axis=-1, keepdims=True)
    zc = z - mu
    var = jnp.mean(zc * zc, axis=-1, keepdims=True)
    return zc * lax.rsqrt(var + LN_EPS) * g + b


def _dot(a, b):
    return jnp.dot(a, b, preferred_element_type=F32)


def _dot_nt(a, b):
    return lax.dot_general(a, b, (((1,), (1,)), ((), ())), preferred_element_type=F32)


def _ada_kernel(c_ref, w_ref, b_ref, o_ref):
    s = _silu(c_ref[...])
    o_ref[0] = jnp.dot(s, w_ref[0], preferred_element_type=F32,
                       precision=lax.Precision.HIGHEST) + b_ref[0]


def _ada_mods(c, c_ctx, ada_w, ada_b):
    depth, d, d6 = ada_w.shape
    b = c.shape[0]
    cc = jnp.zeros((MOD_ROWS, d), F32).at[:b].set(c).at[b].set(c_ctx)
    tn = d6 // 4
    out = pl.pallas_call(
        _ada_kernel,
        grid=(depth, d6 // tn),
        in_specs=[
            pl.BlockSpec((MOD_ROWS, d), lambda i, j: (0, 0)),
            pl.BlockSpec((1, d, tn), lambda i, j: (i, 0, j)),
            pl.BlockSpec((1, 1, tn), lambda i, j: (i, 0, j)),
        ],
        out_specs=pl.BlockSpec((1, MOD_ROWS, tn), lambda i, j: (i, 0, j)),
        out_shape=jax.ShapeDtypeStruct((depth, MOD_ROWS, d6), F32),
        compiler_params=_params(2),
        name="ada_mods",
    )(cc, ada_w, ada_b.reshape(depth, 1, d6))
    return out.reshape(depth, MOD_ROWS, 6, d)


def _proj_kernel(x_ref, mod_ref, w_ref, cs_ref, *rest, rope, tn):
    if rope:
        cos_ref, sin_ref, o_ref = rest
    else:
        (o_ref,) = rest
    shift = mod_ref[0, 0:1, :]
    scale = mod_ref[0, 1:2, :]
    h = (x_ref[0] * (1.0 + scale) + shift).astype(BF16)
    for j in range(w_ref.shape[1] // tn):
        cols = slice(j * tn, (j + 1) * tn)
        acc = _dot(h, w_ref[:, cols]) * cs_ref[:, cols]
        if rope:
            for s in range(tn // LANES):
                seg = acc[:, s * LANES:(s + 1) * LANES]
                t = (s % 2) * LANES
                cos = cos_ref[:, t:t + LANES]
                sin = sin_ref[:, t:t + LANES]
                rot = seg * cos + pltpu.roll(seg, LANES // 2, 1) * sin
                o_ref[0, :, j * tn + s * LANES:j * tn + (s + 1) * LANES] = rot.astype(BF16)
        else:
            o_ref[0, :, cols] = acc.astype(BF16)


def _project(x, mods, mod_row, w, col_scale, rope=None, tm=1024, tn=1024):
    b, n, d = x.shape
    nout = w.shape[1]
    tm = min(tm, n)
    assert n % tm == 0 and nout % tn == 0
    if mod_row is None:
        mod_map = lambda bi, t: (bi, 0, 0)
    else:
        mod_map = lambda bi, t: (mod_row, 0, 0)
    in_specs = [
        pl.BlockSpec((1, tm, d), lambda bi, t: (bi, t, 0)),
        pl.BlockSpec((1, 6, d), mod_map),
        pl.BlockSpec((d, nout), lambda bi, t: (0, 0)),
        pl.BlockSpec((1, nout), lambda bi, t: (0, 0)),
    ]
    args = [x, mods, w, col_scale]
    if rope is not None:
        cos_t, sin_t = rope
        hw = cos_t.shape[1]
        assert tn % hw == 0
        in_specs += [pl.BlockSpec((tm, hw), lambda bi, t: (t, 0))] * 2
        args += [cos_t, sin_t]
    return pl.pallas_call(
        functools.partial(_proj_kernel, rope=rope is not None, tn=tn),
        grid=(b, n // tm),
        in_specs=in_specs,
        out_specs=pl.BlockSpec((1, tm, nout), lambda bi, t: (bi, t, 0)),
        out_shape=jax.ShapeDtypeStruct((b, n, nout), BF16),
        compiler_params=_params(2),
        name="proj",
    )(*args)


def _ret_kernel(dec_ref, qc_ref, kc_ref, vc_ref, ql_ref, kl_ref, vl_ref,
                ol_ref, oc_ref, raw_scr, st_scr, intra_scr, qd_scr, kd_scr, *, chunk):
    cs = chunk
    row = lax.broadcasted_iota(jnp.int32, (cs, cs), 0).astype(F32)
    col = lax.broadcasted_iota(jnp.int32, (cs, cs), 1).astype(F32)
    pos = lax.broadcasted_iota(jnp.int32, (cs, 1), 0).astype(F32)
    cdec = []
    for d in range(2):
        lg = -jnp.exp(dec_ref[0, d, 0:1, 0:1])
        rel = (row - col) if d == 0 else (col - row)
        intra_scr[d] = jnp.where(rel >= 0, jnp.exp(lg * jnp.maximum(rel, 0.0)), 0.0)
        if d == 0:
            qd_scr[d] = jnp.exp(lg * (pos + 1.0))
            kd_scr[d] = jnp.exp(lg * (cs - 1.0 - pos))
        else:
            qd_scr[d] = jnp.exp(lg * (cs - pos))
            kd_scr[d] = jnp.exp(lg * pos)
        cdec.append(jnp.exp(lg * cs))
    st_scr[...] = jnp.zeros_like(st_scr)

    def normed(o):
        mu = jnp.mean(o, axis=-1, keepdims=True)
        oc = o - mu
        var = jnp.mean(oc * oc, axis=-1, keepdims=True)
        return (oc * lax.rsqrt(var + LN_EPS)).astype(BF16)

    def segment(q_ref, k_ref, v_ref, o_ref):
        nch = q_ref.shape[1] // cs

        def chunk_out(d, off):
            q = q_ref[0, pl.ds(off, cs), :]
            k = k_ref[0, pl.ds(off, cs), :]
            v = v_ref[0, pl.ds(off, cs), :]
            st = st_scr[d]
            s = _dot_nt(q, k) * intra_scr[d]
            o = _dot(s.astype(BF16), v) + qd_scr[d] * _dot(q, st.astype(BF16))
            kdt = (k.astype(F32) * kd_scr[d]).T.astype(BF16)
            st_scr[d] = st * cdec[d] + _dot(kdt, v)
            return o

        if nch == 1:
            o_ref[0] = normed(chunk_out(0, 0) + chunk_out(1, 0))
            return
        assert nch % 2 == 0

        def offsets(i):
            return pl.multiple_of(i * cs, cs), pl.multiple_of((nch - 1 - i) * cs, cs)

        def first_half(i, carry):
            off_f, off_b = offsets(i)
            raw_scr[pl.ds(off_f, cs), :] = chunk_out(0, off_f)
            raw_scr[pl.ds(off_b, cs), :] = chunk_out(1, off_b)
            return carry

        def second_half(i, carry):
            off_f, off_b = offsets(i)
            o_ref[0, pl.ds(off_f, cs), :] = normed(chunk_out(0, off_f) + raw_scr[pl.ds(off_f, cs), :])
            o_ref[0, pl.ds(off_b, cs), :] = normed(chunk_out(1, off_b) + raw_scr[pl.ds(off_b, cs), :])
            return carry

        unroll = 2 if (nch // 2) % 2 == 0 else 1
        lax.fori_loop(0, nch // 2, first_half, 0, unroll=unroll)
        lax.fori_loop(nch // 2, nch, second_half, 0, unroll=unroll)

    segment(qc_ref, kc_ref, vc_ref, oc_ref)
    segment(ql_ref, kl_ref, vl_ref, ol_ref)


def _retention(qk_lat, vg_lat, qk_ctx, vg_ctx, decay, chunk=RET_CHUNK):
    b, n_lat, width = qk_lat.shape
    n_ctx = qk_ctx.shape[1]
    nh = RET_HEADS
    dk = width // (2 * nh)
    dv = vg_lat.shape[2] // (2 * nh)
    assert n_ctx % chunk == 0 and n_lat % chunk == 0
    dec = jnp.broadcast_to(decay.astype(F32).T[:, :, None, None], (nh, 2, 8, LANES))

    def specs(n):
        return [
            pl.BlockSpec((1, n, dk), lambda bi, h: (bi, 0, h)),
            pl.BlockSpec((1, n, dk), lambda bi, h: (bi, 0, nh + h)),
            pl.BlockSpec((1, n, dv), lambda bi, h: (bi, 0, h)),
        ]

    return pl.pallas_call(
        functools.partial(_ret_kernel, chunk=chunk),
        grid=(b, nh),
        in_specs=[pl.BlockSpec((1, 2, 8, LANES), lambda bi, h: (h, 0, 0, 0))]
        + specs(n_ctx) + specs(n_lat),
        out_specs=[
            pl.BlockSpec((1, n_lat, dv), lambda bi, h: (bi, 0, h)),
            pl.BlockSpec((1, n_ctx, dv), lambda bi, h: (bi, 0, h)),
        ],
        out_shape=[
            jax.ShapeDtypeStruct((b, n_lat, nh * dv), BF16),
            jax.ShapeDtypeStruct((b, n_ctx, nh * dv), BF16),
        ],
        scratch_shapes=[
            pltpu.VMEM((max(n_ctx, n_lat), dv), F32),
            pltpu.VMEM((2, dk, dv), F32),
            pltpu.VMEM((2, chunk, chunk), F32),
            pltpu.VMEM((2, chunk, 1), F32),
            pltpu.VMEM((2, chunk, 1), F32),
        ],
        compiler_params=_params(2),
        name="retention",
    )(dec, qk_ctx, qk_ctx, vg_ctx, qk_lat, qk_lat, vg_lat)


HEADS_PER_GROUP = MXU_DIM // NAT_HEAD_DIM


def _group_scores(qg, keys, biases, keys_t=()):
    nq = qg.shape[0]
    lane_head = lax.broadcasted_iota(jnp.int32, (nq, MXU_DIM), 1) // NAT_HEAD_DIM
    qm = jnp.concatenate(
        [jnp.where(lane_head == h, qg, 0.0) for h in range(HEADS_PER_GROUP)], axis=0).astype(BF16)
    scores = []
    for k, bias in zip(keys, biases):
        s = _dot_nt(qm, k)
        scores.append(s if bias is None else s + bias)
    for kt in keys_t:
        scores.append(_dot(qm, kt))
    return scores


def _group_values(scores, vals):
    nq = scores[0].shape[0] // HEADS_PER_GROUP
    lane_head = lax.broadcasted_iota(jnp.int32, (nq, MXU_DIM), 1) // NAT_HEAD_DIM

    def lane_fold(parts, op):
        acc = None
        for a in parts:
            for c in range(0, a.shape[1], MXU_DIM):
                blk = a[:, c:c + MXU_DIM]
                acc = blk if acc is None else op(acc, blk)
        return acc

    m = lane_fold(scores, jnp.maximum).max(axis=-1, keepdims=True)
    probs = [jnp.exp2(s - m) for s in scores]
    l = lane_fold(probs, jnp.add).sum(axis=-1, keepdims=True)
    r = None
    for p, v in zip(probs, vals):
        pv = _dot(p.astype(BF16), v)
        r = pv if r is None else r + pv
    r = r * (1.0 / l)
    out = r[0:nq]
    for h in range(1, HEADS_PER_GROUP):
        out = jnp.where(lane_head == h, r[h * nq:(h + 1) * nq], out)
    return out


NAT_BLOCK_KCOLS = 2 * NAT_WIN_COLS
NAT_QBLOCKS = ((0, 24, 0), (24, 16, 16), (40, 24, 32))
NAT_ROW_UNROLL = 4


def _nat_kernel(q_ref, k_ref, v_ref, kct_ref, vc_ref, bias_ref, o_ref, *, rows_per_step, rows):
    kh = NAT_WIN_ROWS
    w = GRID_W
    n_groups = q_ref.shape[2] // MXU_DIM
    t = pl.program_id(1)

    def band(ref, koff, k0, lanes):
        return jnp.concatenate(
            [ref[0, pl.ds(pl.multiple_of(koff + (i * w + k0), NAT_WIN_COLS), NAT_BLOCK_KCOLS), lanes]
             for i in range(kh)], axis=0)

    def score_stage(rr, g):
        r = t * rows_per_step + rr
        start = jnp.clip(r - kh // 2, 0, rows - kh)
        koff = pl.multiple_of(start * w, w)
        qoff = pl.multiple_of(rr * w, w)
        delta = r - start
        lanes = slice(g * MXU_DIM, (g + 1) * MXU_DIM)
        qf = q_ref[0, pl.ds(qoff, w), lanes].astype(F32)
        kcg_t = kct_ref[0, lanes, :]
        scores = []
        for q0, nq, k0 in NAT_QBLOCKS:
            bias = jnp.concatenate(
                [bias_ref[g * HEADS_PER_GROUP + h, delta, q0:q0 + nq, :]
                 for h in range(HEADS_PER_GROUP)], axis=0)
            scores.append(_group_scores(qf[q0:q0 + nq], [band(k_ref, koff, k0, lanes)], [bias],
                                        keys_t=[kcg_t]))
        return scores, koff, qoff, lanes

    def value_stage(scores, koff, qoff, lanes):
        vcg = vc_ref[0, :, lanes]
        outs = [_group_values(s, [band(v_ref, koff, k0, lanes), vcg])
                for s, (q0, nq, k0) in zip(scores, NAT_QBLOCKS)]
        o_ref[0, pl.ds(qoff, w), lanes] = jnp.concatenate(outs, axis=0).astype(BF16)

    def rows_body(it, carry):
        units = [(it * NAT_ROW_UNROLL + j, g) for j in range(NAT_ROW_UNROLL)
                 for g in range(n_groups)]
        pending = None
        for rr, g in units:
            cur = score_stage(rr, g)
            if pending is not None:
                value_stage(*pending)
            pending = cur
        value_stage(*pending)
        return carry

    lax.fori_loop(0, rows_per_step // NAT_ROW_UNROLL, rows_body, 0)


def _bias_window_kernel(t_ref, o_ref):
    kh = NAT_WIN_ROWS
    for delta in range(kh):
        lo = (kh - 1 - delta) * NAT_BLOCK_KCOLS
        o_ref[:, delta] = t_ref[:, :, lo:lo + kh * NAT_BLOCK_KCOLS]


def _nat_bias_table(rpb, heads_per_step=4):
    kw, kh = NAT_WIN_COLS, NAT_WIN_ROWS
    nh = rpb.shape[0]
    cols = np.arange(GRID_W)
    col_start = np.clip(cols - kw // 2, 0, GRID_W - kw)
    k0_of_q = np.zeros((GRID_W,), np.int64)
    for q0, nq, k0 in NAT_QBLOCKS:
        k0_of_q[q0:q0 + nq] = k0
    kc = k0_of_q[:, None] + np.arange(NAT_BLOCK_KCOLS)[None, :]
    valid = (kc >= col_start[:, None]) & (kc < col_start[:, None] + kw)
    dc_idx = np.clip(kc - cols[:, None] + kw - 1, 0, 2 * kw - 2)
    rel = rpb.astype(F32)[:, :, dc_idx]
    rel = jnp.where(valid[None, None], rel * LOG2E, -jnp.inf)
    rel = rel.transpose(0, 2, 1, 3).reshape(nh, GRID_W, (2 * kh - 1) * NAT_BLOCK_KCOLS)
    rel = jnp.pad(rel, ((0, 0), (0, 0), (0, NAT_BLOCK_KCOLS)))
    return pl.pallas_call(
        _bias_window_kernel,
        grid=(nh // heads_per_step,),
        in_specs=[pl.BlockSpec((heads_per_step, GRID_W, rel.shape[2]), lambda i: (i, 0, 0))],
        out_specs=pl.BlockSpec((heads_per_step, kh, GRID_W, kh * NAT_BLOCK_KCOLS),
                               lambda i: (i, 0, 0, 0)),
        out_shape=jax.ShapeDtypeStruct((nh, kh, GRID_W, kh * NAT_BLOCK_KCOLS), F32),
        compiler_params=_params(1),
        name="nat_bias_windows",
    )(rel)


def _nat_attention(p_lat, p_ctx, bias2, rows_per_step=8):
    b, n, d3 = p_lat.shape
    d = d3 // 3
    n_ctx = p_ctx.shape[1]
    rows = n // GRID_W
    assert rows % rows_per_step == 0 and rows >= NAT_WIN_ROWS
    tq = rows_per_step * GRID_W
    kc_t = p_ctx[:, :, d:2 * d].transpose(0, 2, 1)
    return pl.pallas_call(
        functools.partial(_nat_kernel, rows_per_step=rows_per_step, rows=rows),
        grid=(b, rows // rows_per_step),
        in_specs=[
            pl.BlockSpec((1, tq, d), lambda bi, t: (bi, t, 0)),
            pl.BlockSpec((1, n, d), lambda bi, t: (bi, 0, 1), pipeline_mode=pl.Buffered(1)),
            pl.BlockSpec((1, n, d), lambda bi, t: (bi, 0, 2), pipeline_mode=pl.Buffered(1)),
            pl.BlockSpec((1, d, n_ctx), lambda bi, t: (bi, 0, 0)),
            pl.BlockSpec((1, n_ctx, d), lambda bi, t: (bi, 0, 2)),
            pl.BlockSpec(bias2.shape, lambda bi, t: (0, 0, 0, 0)),
        ],
        out_specs=pl.BlockSpec((1, tq, d), lambda bi, t: (bi, t, 0)),
        out_shape=jax.ShapeDtypeStruct((b, n, d), BF16),
        compiler_params=_params(2),
        name="nat_attention",
    )(p_lat, p_lat, p_lat, kc_t, p_ctx, bias2)


def _ctx_attn_kernel(q_ref, k_ref, v_ref, o_ref, *, tq):
    n_groups = q_ref.shape[2] // MXU_DIM
    for qb in range(q_ref.shape[1] // tq):
        for g in range(n_groups):
            lanes = slice(g * MXU_DIM, (g + 1) * MXU_DIM)
            scores = _group_scores(q_ref[0, qb * tq:(qb + 1) * tq, lanes].astype(F32),
                                   [k_ref[0, :, lanes]], [None])
            out = _group_values(scores, [v_ref[0, :, lanes]])
            o_ref[0, qb * tq:(qb + 1) * tq, lanes] = out.astype(BF16)


def _ctx_attention(p_ctx, tq=64):
    b, n_ctx, d3 = p_ctx.shape
    d = d3 // 3
    return pl.pallas_call(
        functools.partial(_ctx_attn_kernel, tq=tq),
        grid=(b,),
        in_specs=[pl.BlockSpec((1, n_ctx, d), lambda bi, c=c: (bi, 0, c)) for c in range(3)],
        out_specs=pl.BlockSpec((1, n_ctx, d), lambda bi: (bi, 0, 0)),
        out_shape=jax.ShapeDtypeStruct((b, n_ctx, d), BF16),
        compiler_params=_params(1),
        name="ctx_attention",
    )(p_ctx, p_ctx, p_ctx)


def _post_kernel(a_ref, *rest, gated, mlp_chunk, n_sub):
    if gated:
        g_ref, x_ref, mod_ref, wo_ref, w1_ref, w2_ref, ln_ref, o_ref = rest
    else:
        x_ref, mod_ref, wo_ref, w1_ref, w2_ref, ln_ref, o_ref = rest
    gate1 = mod_ref[0, 2:3, :]
    shift2 = mod_ref[0, 3:4, :]
    scale2 = mod_ref[0, 4:5, :]
    gate2 = mod_ref[0, 5:6, :]
    hidden = w1_ref.shape[1]
    sub = x_ref.shape[1] // n_sub
    rows = [slice(s * sub, (s + 1) * sub) for s in range(n_sub)]

    def out_proj(s):
        a = a_ref[0, rows[s], :]
        if gated:
            a = (a.astype(F32) * _silu(g_ref[0, rows[s], :].astype(F32))).astype(BF16)
        return _dot(a, wo_ref[...])

    def norm1(s, y):
        x1 = _layer_norm(DN_ALPHA * x_ref[0, rows[s], :] + gate1 * y,
                         ln_ref[0:1, :], ln_ref[1:2, :])
        return x1, (x1 * (1.0 + scale2) + shift2).astype(BF16)

    def mlp(h):
        acc = None
        for c in range(hidden // mlp_chunk):
            cols = slice(c * mlp_chunk, (c + 1) * mlp_chunk)
            u = jnp.maximum(_dot(h, w1_ref[:, cols]), 0.0)
            part = _dot((u * u).astype(BF16), w2_ref[cols, :])
            acc = part if acc is None else acc + part
        return acc

    ys = [out_proj(s) for s in range(n_sub)]
    x1, h = norm1(0, ys[0])
    for s in range(n_sub):
        acc = mlp(h)
        x1_done = x1
        if s + 1 < n_sub:
            x1, h = norm1(s + 1, ys[s + 1])
        o_ref[0, rows[s], :] = _layer_norm(DN_ALPHA * x1_done + gate2 * acc,
                                           ln_ref[2:3, :], ln_ref[3:4, :])


def _post(a, gate_src, gate_blk, x, mods, mod_row, wo, w1, w2, ln, tm=512, mlp_chunk=1024,
          n_sub=2):
    b, n, d = x.shape
    ka = a.shape[2]
    tm = min(tm, n)
    assert n % tm == 0
    gated = gate_src is not None
    if mod_row is None:
        mod_map = lambda bi, t: (bi, 0, 0)
    else:
        mod_map = lambda bi, t: (mod_row, 0, 0)
    in_specs = [pl.BlockSpec((1, tm, ka), lambda bi, t: (bi, t, 0))]
    args = [a]
    if gated:
        in_specs.append(pl.BlockSpec((1, tm, ka), lambda bi, t: (bi, t, gate_blk)))
        args.append(gate_src)
    in_specs += [
        pl.BlockSpec((1, tm, d), lambda bi, t: (bi, t, 0)),
        pl.BlockSpec((1, 6, d), mod_map),
        pl.BlockSpec(wo.shape, lambda bi, t: (0, 0)),
        pl.BlockSpec(w1.shape, lambda bi, t: (0, 0)),
        pl.BlockSpec(w2.shape, lambda bi, t: (0, 0)),
        pl.BlockSpec(ln.shape, lambda bi, t: (0, 0)),
    ]
    args += [x, mods, wo, w1, w2, ln]
    return pl.pallas_call(
        functools.partial(_post_kernel, gated=gated, mlp_chunk=mlp_chunk, n_sub=n_sub),
        grid=(b, n // tm),
        in_specs=in_specs,
        out_specs=pl.BlockSpec((1, tm, d), lambda bi, t: (bi, t, 0)),
        out_shape=jax.ShapeDtypeStruct((b, n, d), F32),
        compiler_params=_params(2),
        name="post",
    )(*args)


def _rope_tables(n, head_dim):
    axis_dim = head_dim // 2
    t = jnp.arange(n)
    row = (t // GRID_W).astype(F32)
    col = (t % GRID_W).astype(F32)
    inv = 1.0 / (ROPE_BASE ** (jnp.arange(0, axis_dim, 2, dtype=F32) / axis_dim))
    ar = row[:, None] * inv
    ac = col[:, None] * inv
    cos = jnp.concatenate([jnp.cos(ar), jnp.cos(ar), jnp.cos(ac), jnp.cos(ac)], axis=-1)
    sin = jnp.concatenate([-jnp.sin(ar), jnp.sin(ar), -jnp.sin(ac), jnp.sin(ac)], axis=-1)
    return cos, sin


def kernel(x, c, ctx, c_ctx, ada_w, ada_b, ret_w_in, ret_w_o, ret_decay,
           nat_w_in, nat_w_o, nat_rpb, mlp_w1, mlp_w2, ln_g, ln_b):
    b, n, d = x.shape
    n_ctx = ctx.shape[1]
    mods_all = _ada_mods(c, c_ctx, ada_w, ada_b)
    ctx_row = b
    ctx = ctx.reshape(1, b * n_ctx, d)

    qk_w = d
    dk = qk_w // RET_HEADS
    rope = _rope_tables(n, dk)
    qk_cs = jnp.concatenate([jnp.ones((qk_w,), F32), jnp.full((qk_w,), dk ** -0.5, F32)])[None, :]
    vg_cs = jnp.ones((1, ret_w_in.shape[2] - 2 * qk_w), F32)
    nat_cs = jnp.concatenate([jnp.full((d,), NAT_HEAD_DIM ** -0.5 * LOG2E, F32),
                              jnp.ones((2 * d,), F32)])[None, :]

    for i in range(DEPTH):
        last = i == DEPTH - 1
        j = i // N_MIXERS
        mods = mods_all[i]
        ln = jnp.stack([ln_g[i, 0], ln_b[i, 0], ln_g[i, 1], ln_b[i, 1]])
        w1 = mlp_w1[i].astype(BF16)
        w2 = mlp_w2[i].astype(BF16)
        if i % N_MIXERS == 0:
            w_qk = ret_w_in[j, :, :2 * qk_w].astype(BF16)
            w_vg = ret_w_in[j, :, 2 * qk_w:].astype(BF16)
            wo = ret_w_o[j].astype(BF16)
            qk_lat = _project(x, mods, None, w_qk, qk_cs, rope=rope)
            vg_lat = _project(x, mods, None, w_vg, vg_cs)
            qk_ctx = _project(ctx, mods, ctx_row, w_qk, qk_cs)
            vg_ctx = _project(ctx, mods, ctx_row, w_vg, vg_cs)
            a_lat, a_ctx = _retention(qk_lat, vg_lat, qk_ctx.reshape(b, n_ctx, -1),
                                      vg_ctx.reshape(b, n_ctx, -1), ret_decay[j])
            gate_blk = 1
            g_lat, g_ctx = vg_lat, vg_ctx
        else:
            w_in = nat_w_in[j].astype(BF16)
            wo = nat_w_o[j].astype(BF16)
            p_lat = _project(x, mods, None, w_in, nat_cs)
            p_ctx = _project(ctx, mods, ctx_row, w_in, nat_cs).reshape(b, n_ctx, -1)
            a_lat = _nat_attention(p_lat, p_ctx, _nat_bias_table(nat_rpb[j]))
            a_ctx = None if last else _ctx_attention(p_ctx)
            gate_blk = 0
            g_lat = g_ctx = None
        x = _post(a_lat, g_lat, gate_blk, x, mods, None, wo, w1, w2, ln)
        if not last:
            ctx = _post(a_ctx.reshape(1, b * n_ctx, -1), g_ctx, gate_blk, ctx, mods, ctx_row,
                        wo, w1, w2, ln)
    return x
```
